```python
import jax, jax.numpy as jnp
from jax import lax
import numpy as np

D_MODEL = 1024
BATCH = 8
SEQ = 4096
DEPTH = 1

RET_HEADS = 4
RET_DK = 128
RET_DV = 256
RET_CHUNK = 128
RET_QK = RET_HEADS * RET_DK
RET_V = RET_HEADS * RET_DV
ROPE_BASE = 10000.0
CONV_WIDTH = 1024
CONV_K = 3
IN_SPLITS = (RET_QK, RET_QK, RET_V, RET_V, CONV_WIDTH, CONV_WIDTH, CONV_WIDTH, D_MODEL, D_MODEL)
IN_WIDTH = 2 * RET_QK + 2 * RET_V + 3 * CONV_WIDTH + 2 * D_MODEL
PEER_HEADS = 8
PEER_NKEYS = 128
PEER_NEXPERTS = PEER_NKEYS * PEER_NKEYS
PEER_QDIM = 256
PEER_HALF = PEER_QDIM // 2
PEER_TOPK = 16
PEER_TOK_BLOCK = 128
PLE_DIM = 256
EPS = 1e-6

kernel_name = "hybrid_retention_shortconv_peer_block"


def rms_norm(x, g):
    xf = x.astype(jnp.float32)
    y = xf * lax.rsqrt(jnp.mean(xf * xf, axis=-1, keepdims=True) + EPS)
    return (y * g.astype(jnp.float32)).astype(x.dtype)


def rotary(x):
    s, half = x.shape[1], x.shape[-1] // 2
    inv = ROPE_BASE ** (-jnp.arange(half, dtype=jnp.float32) / half)
    ang = jnp.arange(s, dtype=jnp.float32)[:, None] * inv[None, :]
    cos = jnp.cos(ang)[None, :, None, :]
    sin = jnp.sin(ang)[None, :, None, :]
    x1, x2 = x[..., :half], x[..., half:]
    out = jnp.concatenate([x1 * cos - x2 * sin, x1 * sin + x2 * cos], axis=-1)
    return out.astype(x.dtype)


def chunkwise_retention(q, k, v):
    b, s, h, dk = q.shape
    dv = v.shape[-1]
    c = RET_CHUNK
    n = s // c
    log_gamma = jnp.log1p(-(2.0 ** (-5.0 - jnp.arange(h, dtype=jnp.float32))))
    pos = jnp.arange(c, dtype=jnp.float32)
    diff = pos[:, None] - pos[None, :]
    causal = diff >= 0
    decay = jnp.where(causal[None], jnp.exp(log_gamma[:, None, None] * jnp.where(causal, diff, 0.0)[None]), 0.0)
    xi = jnp.exp(log_gamma[:, None] * (pos[None, :] + 1.0))
    zeta = jnp.exp(log_gamma[:, None] * (c - 1.0 - pos[None, :]))
    chunk_decay = jnp.exp(log_gamma * c)
    qc = q.reshape(b, n, c, h, dk)
    kc = k.reshape(b, n, c, h, dk)
    vc = v.reshape(b, n, c, h, dv)
    scores = jnp.einsum('bnihd,bnjhd->bnhij', qc, kc).astype(jnp.float32) * decay[None, None]
    intra = jnp.einsum('bnhij,bnjhe->bnihe', scores, vc.astype(jnp.float32))
    kv = jnp.einsum('bnjhd,hj,bnjhe->bnhde', kc.astype(jnp.float32), zeta, vc.astype(jnp.float32))

    def step(state, kv_n):
        return state * chunk_decay[None, :, None, None] + kv_n, state

    init = jnp.zeros((b, h, dk, dv), jnp.float32)
    _, prev = lax.scan(step, init, jnp.moveaxis(kv, 1, 0))
    prev = jnp.moveaxis(prev, 0, 1)
    cross = jnp.einsum('bnihd,hi,bnhde->bnihe', qc.astype(jnp.float32), xi, prev)
    return (intra + cross).reshape(b, s, h, dv)


def head_group_norm(y):
    mu = jnp.mean(y, axis=-1, keepdims=True)
    yc = y - mu
    return yc * lax.rsqrt(jnp.mean(yc * yc, axis=-1, keepdims=True) + EPS)


def causal_short_conv(z, w, bias):
    width = z.shape[-1]
    y = lax.conv_general_dilated(z, w[:, None, :].astype(z.dtype), window_strides=(1,),
                                 padding=[(CONV_K - 1, 0)],
                                 dimension_numbers=('NWC', 'WIO', 'NWC'),
                                 feature_group_count=width)
    return y + bias.astype(z.dtype)


def peer_ffn(a, w_q, k1, k2, u, v):
    b, s, d = a.shape
    t = b * s
    af = a.reshape(t, d)
    q = (af @ w_q).reshape(t, PEER_HEADS, 2, PEER_HALF)
    s1 = jnp.einsum('thd,hnd->thn', q[:, :, 0], k1).astype(jnp.float32)
    s2 = jnp.einsum('thd,hnd->thn', q[:, :, 1], k2).astype(jnp.float32)
    v1, i1 = lax.top_k(s1, PEER_TOPK)
    v2, i2 = lax.top_k(s2, PEER_TOPK)
    cand_s = (v1[..., :, None] + v2[..., None, :]).reshape(t, PEER_HEADS, PEER_TOPK * PEER_TOPK)
    cand_i = (i1[..., :, None] * PEER_NKEYS + i2[..., None, :]).reshape(t, PEER_HEADS, PEER_TOPK * PEER_TOPK)
    top_s, top_pos = lax.top_k(cand_s, PEER_TOPK)
    idx = jnp.take_along_axis(cand_i, top_pos, axis=-1)
    gate = jax.nn.softmax(top_s, axis=-1)
    n_sel = PEER_HEADS * PEER_TOPK
    nb = t // PEER_TOK_BLOCK
    idx = idx.reshape(nb, PEER_TOK_BLOCK, n_sel)
    gate = gate.reshape(nb, PEER_TOK_BLOCK, n_sel)
    xb_all = af.reshape(nb, PEER_TOK_BLOCK, d)

    def block(args):
        xb, ib, gb = args
        ue = jnp.take(u, ib, axis=0)
        act = jax.nn.gelu(jnp.einsum('td,ted->te', xb, ue).astype(jnp.float32), approximate=False)
        ve = jnp.take(v, ib, axis=0)
        return jnp.einsum('te,ted->td', (gb * act).astype(xb.dtype), ve)

    out = lax.map(block, (xb_all, idx, gate))
    return out.reshape(b, s, d)


def setup_inputs(seed: int = 0) -> dict:
    key = jax.random.key(seed)
    ks = jax.random.split(key, 24)
    f32 = jnp.float32
    nrm = lambda k, shape, scale: jax.random.normal(k, shape, f32) * scale
    return {
        "x": nrm(ks[0], (BATCH, SEQ, D_MODEL), 1.0),
        "p": nrm(ks[1], (DEPTH, BATCH, SEQ, PLE_DIM), 1.0),
        "g_mix": 1.0 + nrm(ks[2], (DEPTH, D_MODEL), 0.01),
        "w_in": nrm(ks[3], (DEPTH, D_MODEL, IN_WIDTH), D_MODEL ** -0.5),
        "conv_w": nrm(ks[4], (DEPTH, CONV_K, CONV_WIDTH), CONV_K ** -0.5),
        "conv_b": nrm(ks[5], (DEPTH, CONV_WIDTH), 0.01),
        "w_ret_up": nrm(ks[6], (DEPTH, RET_V, D_MODEL), RET_V ** -0.5),
        "w_conv_up": nrm(ks[7], (DEPTH, CONV_WIDTH, D_MODEL), CONV_WIDTH ** -0.5),
        "w_out": nrm(ks[8], (DEPTH, D_MODEL, D_MODEL), D_MODEL ** -0.5),
        "g_ffn": 1.0 + nrm(ks[9], (DEPTH, D_MODEL), 0.01),
        "w_peer_q": nrm(ks[10], (DEPTH, D_MODEL, PEER_HEADS * PEER_QDIM), D_MODEL ** -0.5),
        "peer_k1": nrm(ks[11], (DEPTH, PEER_HEADS, PEER_NKEYS, PEER_HALF), PEER_HALF ** -0.5),
        "peer_k2": nrm(ks[12], (DEPTH, PEER_HEADS, PEER_NKEYS, PEER_HALF), PEER_HALF ** -0.5),
        "peer_u": nrm(ks[13], (DEPTH, PEER_NEXPERTS, D_MODEL), D_MODEL ** -0.5),
        "peer_v": nrm(ks[14], (DEPTH, PEER_NEXPERTS, D_MODEL), 0.5),
        "g_ple": 1.0 + nrm(ks[15], (DEPTH, D_MODEL), 0.01),
        "w_ple_proj": nrm(ks[16], (DEPTH, PLE_DIM, D_MODEL), PLE_DIM ** -0.5),
        "w_ple_gate": nrm(ks[17], (DEPTH, D_MODEL, D_MODEL), D_MODEL ** -0.5),
        "g_final": 1.0 + nrm(ks[18], (D_MODEL,), 0.01),
    }


def reference(x, p, g_mix, w_in, conv_w, conv_b, w_ret_up, w_conv_up, w_out, g_ffn,
              w_peer_q, peer_k1, peer_k2, peer_u, peer_v, g_ple, w_ple_proj, w_ple_gate, g_final):
    b, s, d = x.shape
    offsets = np.cumsum(IN_SPLITS)[:-1].tolist()
    h = x
    for i in range(DEPTH):
        a = rms_norm(h, g_mix[i])
        proj = a @ w_in[i]
        q, k, v, g_ret, cb, cc, ch, gate_r, gate_c = jnp.split(proj, offsets, axis=-1)
        q = rotary(q.reshape(b, s, RET_HEADS, RET_DK))
        k = rotary(k.reshape(b, s, RET_HEADS, RET_DK)) * (RET_DK ** -0.5)
        v = v.reshape(b, s, RET_HEADS, RET_DV)
        ret = head_group_norm(chunkwise_retention(q, k, v)).reshape(b, s, RET_V)
        ret = (jax.nn.silu(g_ret.astype(jnp.float32)) * ret).astype(x.dtype)
        y_ret = ret @ w_ret_up[i]
        z = causal_short_conv(cc * ch, conv_w[i], conv_b[i])
        y_conv = (cb * z) @ w_conv_up[i]
        merged = jax.nn.sigmoid(gate_r) * y_ret + jax.nn.sigmoid(gate_c) * y_conv
        h = h + merged @ w_out[i]
        h = h + peer_ffn(rms_norm(h, g_ffn[i]), w_peer_q[i], peer_k1[i], peer_k2[i], peer_u[i], peer_v[i])
        ple = p[i] @ w_ple_proj[i]
        h = h + jax.nn.sigmoid(rms_norm(h, g_ple[i]) @ w_ple_gate[i]) * ple
    return rms_norm(h, g_final)
```

```python
import functools

import numpy as np
import jax
import jax.numpy as jnp
from jax import lax
from jax.experimental import pallas as pl
from jax.experimental.pallas import tpu as pltpu

F32 = jnp.float32
BF16 = jnp.bfloat16
EPS = 1e-6

RET_HEADS = 4
RET_DK = 128
RET_DV = 256
RET_CHUNK = 128
ROPE_BASE = 10000.0
CONV_K = 3
PEER_HEADS = 8
PEER_NKEYS = 128
PEER_TOPK = 16

VMEM_LIMIT_BYTES = 56 * 1024 * 1024
LANES = 128


def _params(*sem):
    return pltpu.CompilerParams(dimension_semantics=sem, vmem_limit_bytes=VMEM_LIMIT_BYTES)


def _rms(x, g):
    return x * lax.rsqrt(jnp.mean(x * x, axis=-1, keepdims=True) + EPS) * g


def _dot(a, b):
    return jnp.dot(a, b, preferred_element_type=F32)


def _inproj_kernel(x_ref, g_ref, w_ref, o_ref):
    a = _rms(x_ref[...], g_ref[...]).astype(BF16)
    o_ref[...] = _dot(a, w_ref[...]).astype(o_ref.dtype)


def _inproj(x, g, w, *, tt=512, tn=2048):
    t, d = x.shape
    n = w.shape[1]
    return pl.pallas_call(
        _inproj_kernel,
        grid=(n // tn, t // tt),
        in_specs=[
            pl.BlockSpec((tt, d), lambda j, i: (i, 0)),
            pl.BlockSpec((1, d), lambda j, i: (0, 0)),
            pl.BlockSpec((d, tn), lambda j, i: (0, j)),
        ],
        out_specs=pl.BlockSpec((tt, tn), lambda j, i: (i, j)),
        out_shape=jax.ShapeDtypeStruct((t, n), BF16),
        compiler_params=_params("parallel", "parallel"),
        name="inproj",
    )(x, g, w)


def _ret_kernel(q_ref, k_ref, v_ref, g_ref, cos_ref, sin_ref, dec_ref, xi_ref, zeta_ref, o_ref, st_ref, *, chunk):
    st_ref[...] = jnp.zeros_like(st_ref)
    scale = RET_DK ** -0.5
    half = RET_DK // 2

    def body(c, carry):
        r = pl.ds(pl.multiple_of(c * chunk, chunk), chunk)
        cosv = cos_ref[r, :]
        sinv = sin_ref[r, :]
        q = q_ref[r, :].astype(F32)
        k = k_ref[r, :].astype(F32)
        q = q * cosv + pltpu.roll(q, half, 1) * sinv
        k = (k * cosv + pltpu.roll(k, half, 1) * sinv) * scale
        qb = q.astype(BF16)
        kb = k.astype(BF16)
        v = v_ref[r, :]
        s = lax.dot_general(qb, kb, (((1,), (1,)), ((), ())), preferred_element_type=F32) * dec_ref[0]
        intra = _dot(s.astype(BF16), v)
        st = st_ref[...]
        cross = _dot(qb, st.astype(BF16)) * xi_ref[0]
        vz = (v.astype(F32) * zeta_ref[0]).astype(BF16)
        kv = lax.dot_general(kb, vz, (((0,), (0,)), ((), ())), preferred_element_type=F32)
        st_ref[...] = st * xi_ref[0, chunk - 1:chunk, :] + kv
        y = intra + cross
        yc = y - jnp.mean(y, axis=-1, keepdims=True)
        yn = yc * lax.rsqrt(jnp.mean(yc * yc, axis=-1, keepdims=True) + EPS)
        g = g_ref[r, :].astype(F32)
        o_ref[r, :] = (g * jax.nn.sigmoid(g) * yn).astype(o_ref.dtype)
        return carry

    lax.fori_loop(0, q_ref.shape[0] // chunk, body, 0)


def _retention(proj, cosf, sinf, dec, xi, zeta, *, batch, seq):
    t = proj.shape[0]
    h, c = RET_HEADS, RET_CHUNK
    qk_blocks = h
    v_off = 2 * h * RET_DK // RET_DV
    g_off = v_off + h
    return pl.pallas_call(
        functools.partial(_ret_kernel, chunk=c),
        grid=(batch, h),
        in_specs=[
            pl.BlockSpec((seq, RET_DK), lambda b, j: (b, j)),
            pl.BlockSpec((seq, RET_DK), lambda b, j: (b, qk_blocks + j)),
            pl.BlockSpec((seq, RET_DV), lambda b, j: (b, v_off + j)),
            pl.BlockSpec((seq, RET_DV), lambda b, j: (b, g_off + j)),
            pl.BlockSpec((seq, RET_DK), lambda b, j: (0, 0)),
            pl.BlockSpec((seq, RET_DK), lambda b, j: (0, 0)),
            pl.BlockSpec((1, c, c), lambda b, j: (j, 0, 0)),
            pl.BlockSpec((1, c, RET_DV), lambda b, j: (j, 0, 0)),
            pl.BlockSpec((1, c, RET_DV), lambda b, j: (j, 0, 0)),
        ],
        out_specs=pl.BlockSpec((seq, RET_DV), lambda b, j: (b, j)),
        out_shape=jax.ShapeDtypeStruct((t, h * RET_DV), BF16),
        scratch_shapes=[pltpu.VMEM((RET_DK, RET_DV), F32)],
        compiler_params=_params("parallel", "parallel"),
        name="retention",
    )(proj, proj, proj, proj, cosf, sinf, dec, xi, zeta)


def _retention_tables(seq):
    h, c = RET_HEADS, RET_CHUNK
    half = RET_DK // 2
    inv = ROPE_BASE ** (-jnp.arange(half, dtype=F32) / half)
    ang = jnp.arange(seq, dtype=F32)[:, None] * inv[None, :]
    cos, sin = jnp.cos(ang), jnp.sin(ang)
    cosf = jnp.concatenate([cos, cos], axis=-1)
    sinf = jnp.concatenate([-sin, sin], axis=-1)
    log_gamma = jnp.log1p(-(2.0 ** (-5.0 - jnp.arange(h, dtype=F32))))
    pos = jnp.arange(c, dtype=F32)
    diff = pos[:, None] - pos[None, :]
    causal = diff >= 0
    dec = jnp.where(causal[None], jnp.exp(log_gamma[:, None, None] * jnp.where(causal, diff, 0.0)[None]), 0.0)
    xi = jnp.exp(log_gamma[:, None] * (pos[None, :] + 1.0))
    zeta = jnp.exp(log_gamma[:, None] * (c - 1.0 - pos[None, :]))
    xi = jnp.broadcast_to(xi[:, :, None], (h, c, RET_DV))
    zeta = jnp.broadcast_to(zeta[:, :, None], (h, c, RET_DV))
    return cosf, sinf, dec, xi, zeta


def _mix_kernel(x_ref, ret_ref, cb_ref, cc_ref, ch_ref, gr_ref, gc_ref, cch_ref, chh_ref, cw_ref, cbias_ref,
                wru_ref, wcu_ref, wo_ref, gffn_ref, wqt_ref, h1_ref, a2t_ref, qt_ref, *, tiles_per_seq):
    i = pl.program_id(0)
    u = cc_ref[...].astype(F32) * ch_ref[...].astype(F32)
    halo = cch_ref[...].astype(F32) * chh_ref[...].astype(F32)
    halo = jnp.where(i % tiles_per_seq == 0, 0.0, halo)
    n_halo = halo.shape[0]
    p1 = halo[n_halo - 1:n_halo, :]
    p2 = halo[n_halo - 2:n_halo - 1, :]
    row = lax.broadcasted_iota(jnp.int32, u.shape, 0)
    u1 = jnp.where(row == 0, p1, pltpu.roll(u, 1, 0))
    u2 = jnp.where(row == 0, p2, jnp.where(row == 1, p1, pltpu.roll(u, 2, 0)))
    cw = cw_ref[...]
    z = cw[0:1, :] * u2 + cw[1:2, :] * u1 + cw[2:3, :] * u + cbias_ref[...]
    y_conv = _dot((cb_ref[...].astype(F32) * z).astype(BF16), wcu_ref[...])
    y_ret = _dot(ret_ref[...], wru_ref[...])
    merged = (jax.nn.sigmoid(gr_ref[...].astype(F32)) * y_ret
              + jax.nn.sigmoid(gc_ref[...].astype(F32)) * y_conv)
    h1 = x_ref[...] + _dot(merged.astype(BF16), wo_ref[...])
    h1_ref[...] = h1
    a2t = _rms(h1, gffn_ref[...]).T.astype(BF16)
    a2t_ref[...] = a2t
    qt_ref[...] = _dot(wqt_ref[...], a2t).astype(qt_ref.dtype)


def _mix(x, ret, proj, conv_w, conv_b, w_ret_up, w_conv_up, w_out, g_ffn, wqt, *, seq, tt=256):
    t, d = x.shape
    w = ret.shape[1]
    nq = wqt.shape[0]
    halo = 16
    blk = proj.shape[1] // w
    cb_i, cc_i, ch_i, gr_i, gc_i = blk - 5, blk - 4, blk - 3, blk - 2, blk - 1

    def tile(col):
        return pl.BlockSpec((tt, w), lambda i: (i, col))

    def halo_spec(col):
        return pl.BlockSpec((halo, w), lambda i: (jnp.maximum(i * (tt // halo) - 1, 0), col))

    def const(shape):
        return pl.BlockSpec(shape, lambda i: (0,) * len(shape))

    return pl.pallas_call(
        functools.partial(_mix_kernel, tiles_per_seq=seq // tt),
        grid=(t // tt,),
        in_specs=[
            pl.BlockSpec((tt, d), lambda i: (i, 0)),
            pl.BlockSpec((tt, w), lambda i: (i, 0)),
            tile(cb_i), tile(cc_i), tile(ch_i), tile(gr_i), tile(gc_i),
            halo_spec(cc_i), halo_spec(ch_i),
            const((CONV_K, w)), const((1, w)),
            const((w, d)), const((w, d)), const((d, d)), const((1, d)), const((nq, d)),
        ],
        out_specs=[
            pl.BlockSpec((tt, d), lambda i: (i, 0)),
            pl.BlockSpec((d, tt), lambda i: (0, i)),
            pl.BlockSpec((nq, tt), lambda i: (0, i)),
        ],
        out_shape=[
            jax.ShapeDtypeStruct((t, d), F32),
            jax.ShapeDtypeStruct((d, t), BF16),
            jax.ShapeDtypeStruct((nq, t), BF16),
        ],
        compiler_params=_params("parallel"),
        name="mix",
    )(x, ret, proj, proj, proj, proj, proj, proj, proj, conv_w, conv_b, w_ret_up, w_conv_up, w_out, g_ffn, wqt)


def _candidate_tables():
    k = PEER_TOPK
    pos, valid = [], []
    for j in range(k):
        pos.append(j)
        valid.append(True)
    for i in range(1, 8):
        for j in range(8):
            pos.append(i * k + j)
            valid.append((i + 1) * (j + 1) <= k)
    for i in range(8, k):
        pos.append(i * k)
        valid.append(True)
    pos = np.asarray(pos, np.float32)
    neg = np.where(np.asarray(valid), 0.0, -np.inf).astype(np.float32)
    return (np.ascontiguousarray(np.broadcast_to(pos[:, None], (pos.size, LANES))),
            np.ascontiguousarray(np.broadcast_to(neg[:, None], (neg.size, LANES))))


def _top_ranks(s):
    k = PEER_TOPK
    nkeys = s.shape[0]
    key = lax.broadcasted_iota(jnp.int32, s.shape, 0)
    row = lax.broadcasted_iota(jnp.int32, (k, s.shape[1]), 0)
    rank = jnp.full(s.shape, float(k), F32)
    vals = jnp.zeros((k, s.shape[1]), F32)
    work = s
    for r in range(k):
        m = jnp.max(work, axis=0, keepdims=True)
        first = jnp.min(jnp.where(work == m, key, nkeys), axis=0, keepdims=True)
        sel = key == first
        rank = jnp.where(sel, float(r), rank)
        work = jnp.where(sel, -jnp.inf, work)
        vals = jnp.where(row == r, m, vals)
    return rank, vals


def _select_kernel(qt_ref, k1_ref, k2_ref, pos_ref, neg_ref, r2_ref, e2_ref, c1_ref, e1_ref, s1_ref, s2_ref):
    k = PEER_TOPK
    half = PEER_NKEYS
    s1_ref[...] = _dot(k1_ref[0], qt_ref[0:half, :])
    s2_ref[...] = _dot(k2_ref[0], qt_ref[half:2 * half, :])
    pos = pos_ref[...]
    neg = neg_ref[...]

    def body(g, carry):
        lanes = pl.ds(pl.multiple_of(g * LANES, LANES), LANES)
        s1 = s1_ref[:, lanes]
        s2 = s2_ref[:, lanes]
        rank1, a = _top_ranks(s1)
        rank2, b = _top_ranks(s2)
        pieces = [a[0:1, :] + b]
        for i in range(1, 8):
            pieces.append(a[i:i + 1, :] + b[0:8, :])
        pieces.append(a[8:k, :] + b[0:1, :])
        cand = jnp.concatenate(pieces, axis=0) + neg
        work = cand
        chosen = jnp.zeros(cand.shape, F32)
        for r in range(k):
            m = jnp.max(work, axis=0, keepdims=True)
            first = jnp.min(jnp.where(work == m, pos, float(k * k)), axis=0, keepdims=True)
            sel = pos == first
            chosen = jnp.where(sel, 1.0, chosen)
            work = jnp.where(sel, -jnp.inf, work)
        top = a[0:1, :] + b[0:1, :]
        z = jnp.sum(jnp.where(chosen > 0.0, jnp.exp(cand - top), 0.0), axis=0, keepdims=True)
        row8 = lax.broadcasted_iota(jnp.int32, (8, LANES), 0)
        cnt_lo = jnp.where(row8 == 0, jnp.sum(chosen[0:k, :], axis=0, keepdims=True), 0.0)
        for i in range(1, 8):
            lo = k + 8 * (i - 1)
            cnt_lo = jnp.where(row8 == i, jnp.sum(chosen[lo:lo + 8, :], axis=0, keepdims=True), cnt_lo)
        cnt = jnp.concatenate([cnt_lo, chosen[k + 56:k + 64, :]], axis=0)
        c1 = jnp.zeros(s1.shape, F32)
        for r in range(k):
            c1 = jnp.where(rank1 == float(r), cnt[r:r + 1, :], c1)
        r2_ref[0, :, lanes] = rank2.astype(r2_ref.dtype)
        c1_ref[0, :, lanes] = c1
        e1_ref[0, :, lanes] = jnp.exp(s1 - a[0:1, :]) / z
        e2_ref[0, :, lanes] = jnp.exp(s2 - b[0:1, :]).astype(e2_ref.dtype)
        return carry

    lax.fori_loop(0, s1_ref.shape[1] // LANES, body, 0)


def _select(qt, k1, k2, *, ts=1024):
    nq, t = qt.shape
    h = PEER_HEADS
    nk = PEER_NKEYS
    pos, neg = _candidate_tables()
    out_spec = pl.BlockSpec((1, nk, ts), lambda i, j: (j, 0, i))
    return pl.pallas_call(
        _select_kernel,
        grid=(t // ts, h),
        in_specs=[
            pl.BlockSpec((nq // h, ts), lambda i, j: (j, i)),
            pl.BlockSpec((1, nk, nk), lambda i, j: (j, 0, 0)),
            pl.BlockSpec((1, nk, nk), lambda i, j: (j, 0, 0)),
            pl.BlockSpec(pos.shape, lambda i, j: (0, 0)),
            pl.BlockSpec(neg.shape, lambda i, j: (0, 0)),
        ],
        out_specs=[out_spec, out_spec, out_spec, out_spec],
        out_shape=[
            jax.ShapeDtypeStruct((h, nk, t), BF16),
            jax.ShapeDtypeStruct((h, nk, t), BF16),
            jax.ShapeDtypeStruct((h, nk, t), F32),
            jax.ShapeDtypeStruct((h, nk, t), F32),
        ],
        scratch_shapes=[pltpu.VMEM((nk, ts), F32), pltpu.VMEM((nk, ts), F32)],
        compiler_params=_params("parallel", "parallel"),
        name="peer_select",
    )(qt, k1, k2, jnp.asarray(pos), jnp.asarray(neg))


def _dense_kernel(a2t_ref, u_ref, vt_ref, r2_ref, e2_ref, c1_ref, e1_ref, h1_ref, o_ref, acc_ref, g_ref, *, keys_per_block):
    j = pl.program_id(1)
    nk = PEER_NKEYS

    @pl.when(j == 0)
    def _():
        acc_ref[...] = jnp.zeros_like(acc_ref)

    act = _dot(u_ref[...], a2t_ref[...])
    for kk in range(keys_per_block):
        i1 = j * keys_per_block + kk
        w = None
        for h in range(PEER_HEADS):
            c1 = c1_ref[h, pl.ds(i1, 1), :].astype(BF16)
            e1 = e1_ref[h, pl.ds(i1, 1), :].astype(BF16)
            term = jnp.where(r2_ref[h] < c1, e2_ref[h], jnp.zeros((), BF16)) * e1
            w = term if w is None else w + term
        a = act[kk * nk:(kk + 1) * nk, :]
        gelu = 0.5 * a * (1.0 + lax.erf(a * (2.0 ** -0.5)))
        g_ref[kk * nk:(kk + 1) * nk, :] = (w.astype(F32) * gelu).astype(BF16)
    acc_ref[...] += _dot(vt_ref[...], g_ref[...])

    @pl.when(j == pl.num_programs(1) - 1)
    def _():
        o_ref[...] = h1_ref[...] + acc_ref[...].T


def _dense(a2t, u, vt, r2, e2, c1, e1, h1, *, tt=512, te=512):
    d, t = a2t.shape
    ne = u.shape[0]
    h, nk = PEER_HEADS, PEER_NKEYS
    tab = pl.BlockSpec((h, nk, tt), lambda i, j: (0, 0, i))
    return pl.pallas_call(
        functools.partial(_dense_kernel, keys_per_block=te // nk),
        grid=(t // tt, ne // te),
        in_specs=[
            pl.BlockSpec((d, tt), lambda i, j: (0, i)),
            pl.BlockSpec((te, d), lambda i, j: (j, 0)),
            pl.BlockSpec((d, te), lambda i, j: (0, j)),
            tab, tab, tab, tab,
            pl.BlockSpec((tt, d), lambda i, j: (i, 0)),
        ],
        out_specs=pl.BlockSpec((tt, d), lambda i, j: (i, 0)),
        out_shape=jax.ShapeDtypeStruct((t, d), F32),
        scratch_shapes=[pltpu.VMEM((d, tt), F32), pltpu.VMEM((te, tt), BF16)],
        compiler_params=_params("parallel", "arbitrary"),
        name="peer_dense",
    )(a2t, u, vt, r2, e2, c1, e1, h1)


def _final_kernel(h_ref, p_ref, gple_ref, wproj_ref, wgate_ref, gfin_ref, o_ref, *, last):
    h2 = h_ref[...]
    ple = _dot(p_ref[...].astype(BF16), wproj_ref[...])
    gate = jax.nn.sigmoid(_dot(_rms(h2, gple_ref[...]).astype(BF16), wgate_ref[...]))
    h3 = h2 + gate * ple
    o_ref[...] = _rms(h3, gfin_ref[...]) if last else h3


def _final(h2, p, g_ple, w_proj, w_gate, g_final, *, last, tt=512):
    t, d = h2.shape
    pd = p.shape[1]
    return pl.pallas_call(
        functools.partial(_final_kernel, last=last),
        grid=(t // tt,),
        in_specs=[
            pl.BlockSpec((tt, d), lambda i: (i, 0)),
            pl.BlockSpec((tt, pd), lambda i: (i, 0)),
            pl.BlockSpec((1, d), lambda i: (0, 0)),
            pl.BlockSpec((pd, d), lambda i: (0, 0)),
            pl.BlockSpec((d, d), lambda i: (0, 0)),
            pl.BlockSpec((1, d), lambda i: (0, 0)),
        ],
        out_specs=pl.BlockSpec((tt, d), lambda i: (i, 0)),
        out_shape=jax.ShapeDtypeStruct((t, d), F32),
        compiler_params=_params("parallel"),
        name="ple_final",
    )(h2, p, g_ple, w_proj, w_gate, g_final)


def kernel(x, p, g_mix, w_in, conv_w, conv_b, w_ret_up, w_conv_up, w_out, g_ffn, w_peer_q, peer_k1, peer_k2, peer_u, peer_v, g_ple, w_ple_proj, w_ple_gate, g_final):
    b, s, d = x.shape
    t = b * s
    depth = w_in.shape[0]
    tables = _retention_tables(s)
    h = x.reshape(t, d)
    for i in range(depth):
        proj = _inproj(h, g_mix[i][None], w_in[i].astype(BF16))
        ret = _retention(proj, *tables, batch=b, seq=s)
        h1, a2t, qt = _mix(h, ret, proj, conv_w[i], conv_b[i][None], w_ret_up[i].astype(BF16),
                           w_conv_up[i].astype(BF16), w_out[i].astype(BF16), g_ffn[i][None],
                           w_peer_q[i].T.astype(BF16), seq=s)
        r2, e2, c1, e1 = _select(qt, peer_k1[i].astype(BF16), peer_k2[i].astype(BF16))
        h2 = _dense(a2t, peer_u[i].astype(BF16), peer_v[i].T.astype(BF16), r2, e2, c1, e1, h1)
        h = _final(h2, p[i].reshape(t, -1), g_ple[i][None], w_ple_proj[i].astype(BF16),
                   w_ple_gate[i].astype(BF16), g_final[None], last=i == depth - 1)
    return h.reshape(b, s, d)
```

```python
import functools

import numpy as np
import jax
import jax.numpy as jnp
from jax import lax
from jax.experimental import pallas as pl
from jax.experimental.pallas import tpu as pltpu

F32 = jnp.float32
BF16 = jnp.bfloat16
EPS = 1e-6

RET_HEADS = 4
RET_DK = 128
RET_DV = 256
RET_CHUNK = 128
ROPE_BASE = 10000.0
CONV_K = 3
PEER_HEADS = 8
PEER_NKEYS = 128
PEER_TOPK = 16

VMEM_LIMIT_BYTES = 56 * 1024 * 1024
LANES = 128


def _params(*sem):
    return pltpu.CompilerParams(dimension_semantics=sem, vmem_limit_bytes=VMEM_LIMIT_BYTES)


def _rms(x, g):
    return x * lax.rsqrt(jnp.mean(x * x, axis=-1, keepdims=True) + EPS) * g


def _dot(a, b):
    return jnp.dot(a, b, preferred_element_type=F32)


def _inproj_kernel(x_ref, g_ref, w_ref, o_ref):
    a = _rms(x_ref[...], g_ref[...]).astype(BF16)
    o_ref[...] = _dot(a, w_ref[...]).astype(o_ref.dtype)


def _inproj(x, g, w, *, tt=512, tn=2048):
    t, d = x.shape
    n = w.shape[1]
    return pl.pallas_call(
        _inproj_kernel,
        grid=(n // tn, t // tt),
        in_specs=[
            pl.BlockSpec((tt, d), lambda j, i: (i, 0)),
            pl.BlockSpec((1, d), lambda j, i: (0, 0)),
            pl.BlockSpec((d, tn), lambda j, i: (0, j)),
        ],
        out_specs=pl.BlockSpec((tt, tn), lambda j, i: (i, j)),
        out_shape=jax.ShapeDtypeStruct((t, n), BF16),
        compiler_params=_params("parallel", "parallel"),
        name="inproj",
    )(x, g, w)


def _ret_kernel(q_ref, k_ref, v_ref, g_ref, cos_ref, sin_ref, dec_ref, xi_ref, zeta_ref, o_ref, st_ref, *, chunk):
    st_ref[...] = jnp.zeros_like(st_ref)
    scale = RET_DK ** -0.5
    half = RET_DK // 2

    def body(c, carry):
        r = pl.ds(pl.multiple_of(c * chunk, chunk), chunk)
        cosv = cos_ref[r, :]
        sinv = sin_ref[r, :]
        q = q_ref[r, :].astype(F32)
        k = k_ref[r, :].astype(F32)
        q = q * cosv + pltpu.roll(q, half, 1) * sinv
        k = (k * cosv + pltpu.roll(k, half, 1) * sinv) * scale
        qb = q.astype(BF16)
        kb = k.astype(BF16)
        v = v_ref[r, :]
        s = lax.dot_general(qb, kb, (((1,), (1,)), ((), ())), preferred_element_type=F32) * dec_ref[0]
        intra = _dot(s.astype(BF16), v)
        st = st_ref[...]
        cross = _dot(qb, st.astype(BF16)) * xi_ref[0]
        vz = (v.astype(F32) * zeta_ref[0]).astype(BF16)
        kv = lax.dot_general(kb, vz, (((0,), (0,)), ((), ())), preferred_element_type=F32)
        st_ref[...] = st * xi_ref[0, chunk - 1:chunk, :] + kv
        y = intra + cross
        yc = y - jnp.mean(y, axis=-1, keepdims=True)
        yn = yc * lax.rsqrt(jnp.mean(yc * yc, axis=-1, keepdims=True) + EPS)
        g = g_ref[r, :].astype(F32)
        o_ref[r, :] = (g * jax.nn.sigmoid(g) * yn).astype(o_ref.dtype)
        return carry

    lax.fori_loop(0, q_ref.shape[0] // chunk, body, 0)


def _retention(proj, cosf, sinf, dec, xi, zeta, *, batch, seq):
    t = proj.shape[0]
    h, c = RET_HEADS, RET_CHUNK
    qk_blocks = h
    v_off = 2 * h * RET_DK // RET_DV
    g_off = v_off + h
    return pl.pallas_call(
        functools.partial(_ret_kernel, chunk=c),
        grid=(batch, h),
        in_specs=[
            pl.BlockSpec((seq, RET_DK), lambda b, j: (b, j)),
            pl.BlockSpec((seq, RET_DK), lambda b, j: (b, qk_blocks + j)),
            pl.BlockSpec((seq, RET_DV), lambda b, j: (b, v_off + j)),
            pl.BlockSpec((seq, RET_DV), lambda b, j: (b, g_off + j)),
            pl.BlockSpec((seq, RET_DK), lambda b, j: (0, 0)),
            pl.BlockSpec((seq, RET_DK), lambda b, j: (0, 0)),
            pl.BlockSpec((1, c, c), lambda b, j: (j, 0, 0)),
            pl.BlockSpec((1, c, RET_DV), lambda b, j: (j, 0, 0)),
            pl.BlockSpec((1, c, RET_DV), lambda b, j: (j, 0, 0)),
        ],
        out_specs=pl.BlockSpec((seq, RET_DV), lambda b, j: (b, j)),
        out_shape=jax.ShapeDtypeStruct((t, h * RET_DV), BF16),
        scratch_shapes=[pltpu.VMEM((RET_DK, RET_DV), F32)],
        compiler_params=_params("parallel", "parallel"),
        name="retention",
    )(proj, proj, proj, proj, cosf, sinf, dec, xi, zeta)


def _retention_tables(seq):
    h, c = RET_HEADS, RET_CHUNK
    half = RET_DK // 2
    inv = ROPE_BASE ** (-jnp.arange(half, dtype=F32) / half)
    ang = jnp.arange(seq, dtype=F32)[:, None] * inv[None, :]
    cos, sin = jnp.cos(ang), jnp.sin(ang)
    cosf = jnp.concatenate([cos, cos], axis=-1)
    sinf = jnp.concatenate([-sin, sin], axis=-1)
    log_gamma = jnp.log1p(-(2.0 ** (-5.0 - jnp.arange(h, dtype=F32))))
    pos = jnp.arange(c, dtype=F32)
    diff = pos[:, None] - pos[None, :]
    causal = diff >= 0
    dec = jnp.where(causal[None], jnp.exp(log_gamma[:, None, None] * jnp.where(causal, diff, 0.0)[None]), 0.0)
    xi = jnp.exp(log_gamma[:, None] * (pos[None, :] + 1.0))
    zeta = jnp.exp(log_gamma[:, None] * (c - 1.0 - pos[None, :]))
    xi = jnp.broadcast_to(xi[:, :, None], (h, c, RET_DV))
    zeta = jnp.broadcast_to(zeta[:, :, None], (h, c, RET_DV))
    return cosf, sinf, dec, xi, zeta


def _mix_kernel(x_ref, ret_ref, cb_ref, cc_ref, ch_ref, gr_ref, gc_ref, cch_ref, chh_ref, cw_ref, cbias_ref,
                wru_ref, wcu_ref, wo_ref, gffn_ref, wqt_ref, h1_ref, a2t_ref, qt_ref, *, tiles_per_seq):
    i = pl.program_id(0)
    u = cc_ref[...].astype(F32) * ch_ref[...].astype(F32)
    halo = cch_ref[...].astype(F32) * chh_ref[...].astype(F32)
    halo = jnp.where(i % tiles_per_seq == 0, 0.0, halo)
    n_halo = halo.shape[0]
    p1 = halo[n_halo - 1:n_halo, :]
    p2 = halo[n_halo - 2:n_halo - 1, :]
    row = lax.broadcasted_iota(jnp.int32, u.shape, 0)
    u1 = jnp.where(row == 0, p1, pltpu.roll(u, 1, 0))
    u2 = jnp.where(row == 0, p2, jnp.where(row == 1, p1, pltpu.roll(u, 2, 0)))
    cw = cw_ref[...]
    z = cw[0:1, :] * u2 + cw[1:2, :] * u1 + cw[2:3, :] * u + cbias_ref[...]
    y_conv = _dot((cb_ref[...].astype(F32) * z).astype(BF16), wcu_ref[...])
    y_ret = _dot(ret_ref[...], wru_ref[...])
    merged = (jax.nn.sigmoid(gr_ref[...].astype(F32)) * y_ret
              + jax.nn.sigmoid(gc_ref[...].astype(F32)) * y_conv)
    h1 = x_ref[...] + _dot(merged.astype(BF16), wo_ref[...])
    h1_ref[...] = h1
    a2t = _rms(h1, gffn_ref[...]).T.astype(BF16)
    a2t_ref[...] = a2t
    qt_ref[...] = _dot(wqt_ref[...], a2t).astype(qt_ref.dtype)


def _mix(x, ret, proj, conv_w, conv_b, w_ret_up, w_conv_up, w_out, g_ffn, wqt, *, seq, tt=256):
    t, d = x.shape
    w = ret.shape[1]
    nq = wqt.shape[0]
    halo = 16
    blk = proj.shape[1] // w
    cb_i, cc_i, ch_i, gr_i, gc_i = blk - 5, blk - 4, blk - 3, blk - 2, blk - 1

    def tile(col):
        return pl.BlockSpec((tt, w), lambda i: (i, col))

    def halo_spec(col):
        return pl.BlockSpec((halo, w), lambda i: (jnp.maximum(i * (tt // halo) - 1, 0), col))

    def const(shape):
        return pl.BlockSpec(shape, lambda i: (0,) * len(shape))

    return pl.pallas_call(
        functools.partial(_mix_kernel, tiles_per_seq=seq // tt),
        grid=(t // tt,),
        in_specs=[
            pl.BlockSpec((tt, d), lambda i: (i, 0)),
            pl.BlockSpec((tt, w), lambda i: (i, 0)),
            tile(cb_i), tile(cc_i), tile(ch_i), tile(gr_i), tile(gc_i),
            halo_spec(cc_i), halo_spec(ch_i),
            const((CONV_K, w)), const((1, w)),
            const((w, d)), const((w, d)), const((d, d)), const((1, d)), const((nq, d)),
        ],
        out_specs=[
            pl.BlockSpec((tt, d), lambda i: (i, 0)),
            pl.BlockSpec((d, tt), lambda i: (0, i)),
            pl.BlockSpec((nq, tt), lambda i: (0, i)),
        ],
        out_shape=[
            jax.ShapeDtypeStruct((t, d), F32),
            jax.ShapeDtypeStruct((d, t), BF16),
            jax.ShapeDtypeStruct((nq, t), BF16),
        ],
        compiler_params=_params("parallel"),
        name="mix",
    )(x, ret, proj, proj, proj, proj, proj, proj, proj, conv_w, conv_b, w_ret_up, w_conv_up, w_out, g_ffn, wqt)


def _candidate_tables():
    k = PEER_TOPK
    pos, valid = [], []
    for j in range(k):
        pos.append(j)
        valid.append(True)
    for i in range(1, 8):
        for j in range(8):
            pos.append(i * k + j)
            valid.append((i + 1) * (j + 1) <= k)
    for i in range(8, k):
        pos.append(i * k)
        valid.append(True)
    pos = np.asarray(pos, np.float32)
    neg = np.where(np.asarray(valid), 0.0, -np.inf).astype(np.float32)
    return (np.ascontiguousarray(np.broadcast_to(pos[:, None], (pos.size, LANES))),
            np.ascontiguousarray(np.broadcast_to(neg[:, None], (neg.size, LANES))))


def _top_ranks(s):
    k = PEER_TOPK
    nkeys = s.shape[0]
    key = lax.broadcasted_iota(jnp.int32, s.shape, 0)
    row = lax.broadcasted_iota(jnp.int32, (k, s.shape[1]), 0)
    rank = jnp.full(s.shape, float(k), F32)
    vals = jnp.zeros((k, s.shape[1]), F32)
    work = s
    for r in range(k):
        m = jnp.max(work, axis=0, keepdims=True)
        first = jnp.min(jnp.where(work == m, key, nkeys), axis=0, keepdims=True)
        sel = key == first
        rank = jnp.where(sel, float(r), rank)
        work = jnp.where(sel, -jnp.inf, work)
        vals = jnp.where(row == r, m, vals)
    return rank, vals


def _sort16_network():
    def merge(lo, hi, r):
        step = r * 2
        if step < hi - lo:
            yield from merge(lo, hi, step)
            yield from merge(lo + r, hi, step)
            yield from [(i, i + r) for i in range(lo + r, hi - r, step)]
        else:
            yield (lo, lo + r)

    def sort(lo, hi):
        if hi - lo >= 1:
            mid = lo + (hi - lo) // 2
            yield from sort(lo, mid)
            yield from sort(mid + 1, hi)
            yield from merge(lo, hi, 1)

    return tuple(sort(0, PEER_TOPK - 1))


def _sublane_allreduce(x, op):
    for shift in (4, 2, 1):
        x = op(x, pltpu.roll(x, shift, 0))
    return x


def _sorted_top_values(s):
    k = PEER_TOPK
    x = [s[8 * i:8 * i + 8, :] for i in range(s.shape[0] // 8)]
    assert len(x) == k
    for i, j in _sort16_network():
        x[i], x[j] = jnp.maximum(x[i], x[j]), jnp.minimum(x[i], x[j])
    for shift in (4, 2, 1):
        p = [pltpu.roll(v, shift, 0) for v in x]
        y = [jnp.maximum(x[r], p[k - 1 - r]) for r in range(k)]
        d = k // 2
        while d:
            for i in range(k):
                if not i & d:
                    y[i], y[i + d] = jnp.maximum(y[i], y[i + d]), jnp.minimum(y[i], y[i + d])
            d //= 2
        x = y
    return x


def _count_leading(b, pred):
    m8 = pred(b[7])
    m4 = pred(jnp.where(m8, b[11], b[3]))
    m2 = pred(jnp.where(m8, jnp.where(m4, b[13], b[9]), jnp.where(m4, b[5], b[1])))
    m1 = pred(jnp.where(m8, jnp.where(m4, jnp.where(m2, b[14], b[12]), jnp.where(m2, b[10], b[8])),
                        jnp.where(m4, jnp.where(m2, b[6], b[4]), jnp.where(m2, b[2], b[0]))))
    cnt = (jnp.where(m8, 8.0, 0.0) + jnp.where(m4, 4.0, 0.0)) + (jnp.where(m2, 2.0, 0.0) + jnp.where(m1, 1.0, 0.0))
    return jnp.where(pred(b[15]), 16.0, cnt)


def _select_fast(s1, s2):
    k = PEER_TOPK
    assert k == 16
    n = s1.shape[1]
    a = _sorted_top_values(s1)
    b = _sorted_top_values(s2)
    row = lax.broadcasted_iota(jnp.int32, (8, n), 0)

    def rows_of(vals):
        out = vals[0]
        for r in range(1, 8):
            out = jnp.where(row == r, vals[r], out)
        return out

    b_lo, b_hi, a_hi = rows_of(b[0:8]), rows_of(b[8:16]), rows_of(a[8:16])
    neg_inf = jnp.float32(-jnp.inf)
    cand = [a[0] + b_lo, a[0] + b_hi]
    for i in range(1, 8):
        cand.append(jnp.where(row < k // (i + 1), a[i] + b_lo, neg_inf))
    cand.append(a_hi + b[0])
    work = list(cand)
    tau = None
    for r in range(k):
        m = work[0]
        for w in work[1:]:
            m = jnp.maximum(m, w)
        m = _sublane_allreduce(m, jnp.maximum)
        if r == k - 1:
            tau = m
        else:
            work = [jnp.where(w == m, neg_inf, w) for w in work]
    top = a[0] + b[0]
    z = None
    for c in cand:
        term = jnp.where(c >= tau, jnp.exp(c - top), 0.0)
        z = term if z is None else z + term
    inv_z = 1.0 / _sublane_allreduce(z, jnp.add)

    rank2, count1, e1, e2 = [], [], [], []
    n_top2 = None
    n_pairs = None
    for i in range(s1.shape[0] // 8):
        x1 = s1[8 * i:8 * i + 8, :]
        x2 = s2[8 * i:8 * i + 8, :]
        r2 = _count_leading(b, lambda t: t > x2)
        c1 = _count_leading(b, lambda t: x1 + t >= tau)
        rank2.append(r2)
        count1.append(c1)
        e1.append(jnp.exp(x1 - a[0]) * inv_z)
        e2.append(jnp.exp(x2 - b[0]))
        in_top = jnp.where(r2 < float(k), 1.0, 0.0)
        n_top2 = in_top if n_top2 is None else n_top2 + in_top
        n_pairs = c1 if n_pairs is None else n_pairs + c1
    n_top2 = _sublane_allreduce(n_top2, jnp.add)
    n_pairs = _sublane_allreduce(n_pairs, jnp.add)
    suspicious = jnp.where((n_top2 != float(k)) | (n_pairs != float(k)), 1.0, 0.0)
    cat = lambda xs: jnp.concatenate(xs, axis=0)
    return cat(rank2), cat(count1), cat(e1), cat(e2), suspicious


def _select_exact(s1, s2, pos, neg):
    k = PEER_TOPK
    rank1, a = _top_ranks(s1)
    rank2, b = _top_ranks(s2)
    pieces = [a[0:1, :] + b]
    for i in range(1, 8):
        pieces.append(a[i:i + 1, :] + b[0:8, :])
    pieces.append(a[8:k, :] + b[0:1, :])
    cand = jnp.concatenate(pieces, axis=0) + neg
    work = cand
    chosen = jnp.zeros(cand.shape, F32)
    for r in range(k):
        m = jnp.max(work, axis=0, keepdims=True)
        first = jnp.min(jnp.where(work == m, pos, float(k * k)), axis=0, keepdims=True)
        sel = pos == first
        chosen = jnp.where(sel, 1.0, chosen)
        work = jnp.where(sel, -jnp.inf, work)
    top = a[0:1, :] + b[0:1, :]
    z = jnp.sum(jnp.where(chosen > 0.0, jnp.exp(cand - top), 0.0), axis=0, keepdims=True)
    row8 = lax.broadcasted_iota(jnp.int32, (8, LANES), 0)
    cnt_lo = jnp.where(row8 == 0, jnp.sum(chosen[0:k, :], axis=0, keepdims=True), 0.0)
    for i in range(1, 8):
        lo = k + 8 * (i - 1)
        cnt_lo = jnp.where(row8 == i, jnp.sum(chosen[lo:lo + 8, :], axis=0, keepdims=True), cnt_lo)
    cnt = jnp.concatenate([cnt_lo, chosen[k + 56:k + 64, :]], axis=0)
    count1 = jnp.zeros(s1.shape, F32)
    for r in range(k):
        count1 = jnp.where(rank1 == float(r), cnt[r:r + 1, :], count1)
    return rank2, count1, jnp.exp(s1 - a[0:1, :]) / z, jnp.exp(s2 - b[0:1, :])


def _select_kernel(qt_ref, k1_ref, k2_ref, pos_ref, neg_ref, r2_ref, e2_ref, c1_ref, e1_ref, s1_ref, s2_ref):
    half = PEER_NKEYS
    s1_ref[...] = _dot(k1_ref[0], qt_ref[0:half, :])
    s2_ref[...] = _dot(k2_ref[0], qt_ref[half:2 * half, :])

    def body(g, carry):
        lanes = pl.ds(pl.multiple_of(g * LANES, LANES), LANES)

        def emit(rank2, count1, e1, e2):
            r2_ref[0, :, lanes] = rank2.astype(r2_ref.dtype)
            c1_ref[0, :, lanes] = count1
            e1_ref[0, :, lanes] = e1
            e2_ref[0, :, lanes] = e2.astype(e2_ref.dtype)

        *tables, suspicious = _select_fast(s1_ref[:, lanes], s2_ref[:, lanes])
        emit(*tables)

        @pl.when(jnp.max(suspicious) > 0.0)
        def _():
            emit(*_select_exact(s1_ref[:, lanes], s2_ref[:, lanes], pos_ref[...], neg_ref[...]))

        return carry

    lax.fori_loop(0, s1_ref.shape[1] // LANES, body, 0)


def _select(qt, k1, k2, *, ts=1024):
    nq, t = qt.shape
    h = PEER_HEADS
    nk = PEER_NKEYS
    pos, neg = _candidate_tables()
    out_spec = pl.BlockSpec((1, nk, ts), lambda i, j: (j, 0, i))
    return pl.pallas_call(
        _select_kernel,
        grid=(t // ts, h),
        in_specs=[
            pl.BlockSpec((nq // h, ts), lambda i, j: (j, i)),
            pl.BlockSpec((1, nk, nk), lambda i, j: (j, 0, 0)),
            pl.BlockSpec((1, nk, nk), lambda i, j: (j, 0, 0)),
            pl.BlockSpec(pos.shape, lambda i, j: (0, 0)),
            pl.BlockSpec(neg.shape, lambda i, j: (0, 0)),
        ],
        out_specs=[out_spec, out_spec, out_spec, out_spec],
        out_shape=[
            jax.ShapeDtypeStruct((h, nk, t), BF16),
            jax.ShapeDtypeStruct((h, nk, t), BF16),
            jax.ShapeDtypeStruct((h, nk, t), F32),
            jax.ShapeDtypeStruct((h, nk, t), F32),
        ],
        scratch_shapes=[pltpu.VMEM((nk, ts), F32), pltpu.VMEM((nk, ts), F32)],
        compiler_params=_params("parallel", "parallel"),
        name="peer_select",
    )(qt, k1, k2, jnp.asarray(pos), jnp.asarray(neg))


def _dense_kernel(a2t_ref, u_ref, vt_ref, r2_ref, e2_ref, c1_ref, e1_ref, h1_ref, o_ref, acc_ref, g_ref, act_ref, *, keys_per_block):
    j = pl.program_id(1)
    nk = PEER_NKEYS
    tt = a2t_ref.shape[1]

    @pl.when(j == 0)
    def _():
        acc_ref[...] = jnp.zeros_like(acc_ref)

    act_ref[...] = _dot(u_ref[...], a2t_ref[...])
    for kk in range(keys_per_block):
        i1 = j * keys_per_block + kk
        w = None
        for h in range(PEER_HEADS):
            c1 = jnp.broadcast_to(c1_ref[h, pl.ds(i1, 1), :], (16, tt)).astype(BF16)
            e1 = jnp.broadcast_to(e1_ref[h, pl.ds(i1, 1), :], (16, tt)).astype(BF16)
            c1 = jnp.concatenate([c1] * (nk // 16), axis=0)
            e1 = jnp.concatenate([e1] * (nk // 16), axis=0)
            term = jnp.where(r2_ref[h] < c1, e2_ref[h], jnp.zeros((), BF16)) * e1
            w = term if w is None else w + term
        rows = slice(kk * nk, (kk + 1) * nk)
        a = act_ref[rows, :]
        gelu = 0.5 * a * (1.0 + lax.erf(a * (2.0 ** -0.5)))
        g_ref[rows, :] = (w.astype(F32) * gelu).astype(BF16)
    acc_ref[...] += _dot(vt_ref[...], g_ref[...])

    @pl.when(j == pl.num_programs(1) - 1)
    def _():
        o_ref[...] = h1_ref[...] + acc_ref[...].T


def _dense(a2t, u, vt, r2, e2, c1, e1, h1, *, tt=512, te=1024):
    d, t = a2t.shape
    ne = u.shape[0]
    h, nk = PEER_HEADS, PEER_NKEYS
    tab = pl.BlockSpec((h, nk, tt), lambda i, j: (0, 0, i))
    return pl.pallas_call(
        functools.partial(_dense_kernel, keys_per_block=te // nk),
        grid=(t // tt, ne // te),
        in_specs=[
            pl.BlockSpec((d, tt), lambda i, j: (0, i)),
            pl.BlockSpec((te, d), lambda i, j: (j, 0)),
            pl.BlockSpec((d, te), lambda i, j: (0, j)),
            tab, tab, tab, tab,
            pl.BlockSpec((tt, d), lambda i, j: (i, 0)),
        ],
        out_specs=pl.BlockSpec((tt, d), lambda i, j: (i, 0)),
        out_shape=jax.ShapeDtypeStruct((t, d), F32),
        scratch_shapes=[pltpu.VMEM((d, tt), F32), pltpu.VMEM((te, tt), BF16), pltpu.VMEM((te, tt), F32)],
        compiler_params=_params("parallel", "arbitrary"),
        name="peer_dense",
    )(a2t, u, vt, r2, e2, c1, e1, h1)


def _final_kernel(h_ref, p_ref, gple_ref, wproj_ref, wgate_ref, gfin_ref, o_ref, *, last):
    h2 = h_ref[...]
    ple = _dot(p_ref[...].astype(BF16), wproj_ref[...])
    gate = jax.nn.sigmoid(_dot(_rms(h2, gple_ref[...]).astype(BF16), wgate_ref[...]))
    h3 = h2 + gate * ple
    o_ref[...] = _rms(h3, gfin_ref[...]) if last else h3


def _final(h2, p, g_ple, w_proj, w_gate, g_final, *, last, tt=512):
    t, d = h2.shape
    pd = p.shape[1]
    return pl.pallas_call(
        functools.partial(_final_kernel, last=last),
        grid=(t // tt,),
        in_specs=[
            pl.BlockSpec((tt, d), lambda i: (i, 0)),
            pl.BlockSpec((tt, pd), lambda i: (i, 0)),
            pl.BlockSpec((1, d), lambda i: (0, 0)),
            pl.BlockSpec((pd, d), lambda i: (0, 0)),
            pl.BlockSpec((d, d), lambda i: (0, 0)),
            pl.BlockSpec((1, d), lambda i: (0, 0)),
        ],
        out_specs=pl.BlockSpec((tt, d), lambda i: (i, 0)),
        out_shape=jax.ShapeDtypeStruct((t, d), F32),
        compiler_params=_params("parallel"),
        name="ple_final",
    )(h2, p, g_ple, w_proj, w_gate, g_final)


def kernel(x, p, g_mix, w_in, conv_w, conv_b, w_ret_up, w_conv_up, w_out, g_ffn, w_peer_q, peer_k1, peer_k2, peer_u, peer_v, g_ple, w_ple_proj, w_ple_gate, g_final):
    b, s, d = x.shape
    t = b * s
    depth = w_in.shape[0]
    tables = _retention_tables(s)
    h = x.reshape(t, d)
    for i in range(depth):
        proj = _inproj(h, g_mix[i][None], w_in[i].astype(BF16))
        ret = _retention(proj, *tables, batch=b, seq=s)
        h1, a2t, qt = _mix(h, ret, proj, conv_w[i], conv_b[i][None], w_ret_up[i].astype(BF16),
                           w_conv_up[i].astype(BF16), w_out[i].astype(BF16), g_ffn[i][None],
                           w_peer_q[i].T.astype(BF16), seq=s)
        r2, e2, c1, e1 = _select(qt, peer_k1[i].astype(BF16), peer_k2[i].astype(BF16))
        h2 = _dense(a2t, peer_u[i].astype(BF16), peer_v[i].T.astype(BF16), r2, e2, c1, e1, h1)
        h = _final(h2, p[i].reshape(t, -1), g_ple[i][None], w_ple_proj[i].astype(BF16),
                   w_ple_gate[i].astype(BF16), g_final[None], last=i == depth - 1)
    return h.reshape(b, s, d)
```

```python
import functools

import numpy as np
import jax
import jax.numpy as jnp
from jax import lax
from jax.experimental import pallas as pl
from jax.experimental.pallas import tpu as pltpu

F32 = jnp.float32
BF16 = jnp.bfloat16
EPS = 1e-6

RET_HEADS = 4
RET_DK = 128
RET_DV = 256
RET_CHUNK = 128
ROPE_BASE = 10000.0
CONV_K = 3
PEER_HEADS = 8
PEER_NKEYS = 128
PEER_TOPK = 16

VMEM_LIMIT_BYTES = 56 * 1024 * 1024
LANES = 128


def _params(*sem):
    return pltpu.CompilerParams(dimension_semantics=sem, vmem_limit_bytes=VMEM_LIMIT_BYTES)


def _rms(x, g):
    return x * lax.rsqrt(jnp.mean(x * x, axis=-1, keepdims=True) + EPS) * g


def _dot(a, b):
    return jnp.dot(a, b, preferred_element_type=F32)


def _inproj_kernel(x_ref, g_ref, w_ref, o_ref):
    a = _rms(x_ref[...], g_ref[...]).astype(BF16)
    o_ref[...] = _dot(a, w_ref[...]).astype(o_ref.dtype)


def _inproj(x, g, w, *, tt=512, tn=2048):
    t, d = x.shape
    n = w.shape[1]
    return pl.pallas_call(
        _inproj_kernel,
        grid=(n // tn, t // tt),
        in_specs=[
            pl.BlockSpec((tt, d), lambda j, i: (i, 0)),
            pl.BlockSpec((1, d), lambda j, i: (0, 0)),
            pl.BlockSpec((d, tn), lambda j, i: (0, j)),
        ],
        out_specs=pl.BlockSpec((tt, tn), lambda j, i: (i, j)),
        out_shape=jax.ShapeDtypeStruct((t, n), BF16),
        compiler_params=_params("parallel", "parallel"),
        name="inproj",
    )(x, g, w)


def _ret_kernel(q_ref, k_ref, v_ref, g_ref, cos_ref, sin_ref, dec_ref, xi_ref, zeta_ref, o_ref, st_ref, *, chunk):
    @pl.when(pl.program_id(1) == 0)
    def _():
        st_ref[...] = jnp.zeros_like(st_ref)

    scale = RET_DK ** -0.5
    half = RET_DK // 2

    def body(c, carry):
        r = pl.ds(pl.multiple_of(c * chunk, chunk), chunk)
        cosv = cos_ref[r, :]
        sinv = sin_ref[r, :]
        for h in range(RET_HEADS):
            qk = slice(h * RET_DK, (h + 1) * RET_DK)
            vv = slice(h * RET_DV, (h + 1) * RET_DV)
            q = q_ref[r, qk].astype(F32)
            k = k_ref[r, qk].astype(F32)
            q = q * cosv + pltpu.roll(q, half, 1) * sinv
            k = (k * cosv + pltpu.roll(k, half, 1) * sinv) * scale
            qb = q.astype(BF16)
            kb = k.astype(BF16)
            v = v_ref[r, vv]
            s = lax.dot_general(qb, kb, (((1,), (1,)), ((), ())), preferred_element_type=F32) * dec_ref[h]
            intra = _dot(s.astype(BF16), v)
            st = st_ref[h]
            cross = _dot(qb, st.astype(BF16)) * xi_ref[h]
            vz = (v.astype(F32) * zeta_ref[h]).astype(BF16)
            kv = lax.dot_general(kb, vz, (((0,), (0,)), ((), ())), preferred_element_type=F32)
            st_ref[h] = st * xi_ref[h, chunk - 1:chunk, :] + kv
            y = intra + cross
            yc = y - jnp.mean(y, axis=-1, keepdims=True)
            yn = yc * lax.rsqrt(jnp.mean(yc * yc, axis=-1, keepdims=True) + EPS)
            g = g_ref[r, vv].astype(F32)
            o_ref[r, vv] = (g * jax.nn.sigmoid(g) * yn).astype(o_ref.dtype)
        return carry

    lax.fori_loop(0, q_ref.shape[0] // chunk, body, 0)


def _retention(proj, cosf, sinf, dec, xi, zeta, *, batch, seq, sb=1024):
    t = proj.shape[0]
    h, c = RET_HEADS, RET_CHUNK
    sb = min(sb, seq)
    nsb = seq // sb
    qk_w, v_w = h * RET_DK, h * RET_DV
    assert 2 * qk_w == v_w

    def rows(col):
        return lambda b, s: (b * nsb + s, col)

    def const(shape):
        return pl.BlockSpec(shape, lambda b, s: (0,) * len(shape))

    return pl.pallas_call(
        functools.partial(_ret_kernel, chunk=c),
        grid=(batch, nsb),
        in_specs=[
            pl.BlockSpec((sb, qk_w), rows(0)),
            pl.BlockSpec((sb, qk_w), rows(1)),
            pl.BlockSpec((sb, v_w), rows(1)),
            pl.BlockSpec((sb, v_w), rows(2)),
            pl.BlockSpec((sb, RET_DK), lambda b, s: (s, 0)),
            pl.BlockSpec((sb, RET_DK), lambda b, s: (s, 0)),
            const((h, c, c)), const((h, c, RET_DV)), const((h, c, RET_DV)),
        ],
        out_specs=pl.BlockSpec((sb, v_w), lambda b, s: (b * nsb + s, 0)),
        out_shape=jax.ShapeDtypeStruct((t, v_w), BF16),
        scratch_shapes=[pltpu.VMEM((h, RET_DK, RET_DV), F32)],
        compiler_params=_params("parallel", "arbitrary"),
        name="retention",
    )(proj, proj, proj, proj, cosf, sinf, dec, xi, zeta)


def _retention_tables(seq):
    h, c = RET_HEADS, RET_CHUNK
    half = RET_DK // 2
    inv = ROPE_BASE ** (-jnp.arange(half, dtype=F32) / half)
    ang = jnp.arange(seq, dtype=F32)[:, None] * inv[None, :]
    cos, sin = jnp.cos(ang), jnp.sin(ang)
    cosf = jnp.concatenate([cos, cos], axis=-1)
    sinf = jnp.concatenate([-sin, sin], axis=-1)
    log_gamma = jnp.log1p(-(2.0 ** (-5.0 - jnp.arange(h, dtype=F32))))
    pos = jnp.arange(c, dtype=F32)
    diff = pos[:, None] - pos[None, :]
    causal = diff >= 0
    dec = jnp.where(causal[None], jnp.exp(log_gamma[:, None, None] * jnp.where(causal, diff, 0.0)[None]), 0.0)
    xi = jnp.exp(log_gamma[:, None] * (pos[None, :] + 1.0))
    zeta = jnp.exp(log_gamma[:, None] * (c - 1.0 - pos[None, :]))
    xi = jnp.broadcast_to(xi[:, :, None], (h, c, RET_DV))
    zeta = jnp.broadcast_to(zeta[:, :, None], (h, c, RET_DV))
    return cosf, sinf, dec, xi, zeta


def _mix_kernel(x_ref, ret_ref, cb_ref, cc_ref, ch_ref, gr_ref, gc_ref, cch_ref, chh_ref, cw_ref, cbias_ref,
                wru_ref, wcu_ref, wo_ref, gffn_ref, wqt_ref, h1_ref, a2t_ref, qt_ref, *, tiles_per_seq):
    i = pl.program_id(0)
    u = cc_ref[...].astype(F32) * ch_ref[...].astype(F32)
    halo = cch_ref[...].astype(F32) * chh_ref[...].astype(F32)
    halo = jnp.where(i % tiles_per_seq == 0, 0.0, halo)
    n_halo = halo.shape[0]
    p1 = halo[n_halo - 1:n_halo, :]
    p2 = halo[n_halo - 2:n_halo - 1, :]
    row = lax.broadcasted_iota(jnp.int32, u.shape, 0)
    u1 = jnp.where(row == 0, p1, pltpu.roll(u, 1, 0))
    u2 = jnp.where(row == 0, p2, jnp.where(row == 1, p1, pltpu.roll(u, 2, 0)))
    cw = cw_ref[...]
    z = cw[0:1, :] * u2 + cw[1:2, :] * u1 + cw[2:3, :] * u + cbias_ref[...]
    y_conv = _dot((cb_ref[...].astype(F32) * z).astype(BF16), wcu_ref[...])
    y_ret = _dot(ret_ref[...], wru_ref[...])
    merged = (jax.nn.sigmoid(gr_ref[...].astype(F32)) * y_ret
              + jax.nn.sigmoid(gc_ref[...].astype(F32)) * y_conv)
    h1 = x_ref[...] + _dot(merged.astype(BF16), wo_ref[...])
    h1_ref[...] = h1
    a2t = _rms(h1, gffn_ref[...]).T.astype(BF16)
    a2t_ref[...] = a2t
    qt_ref[...] = _dot(wqt_ref[...], a2t).astype(qt_ref.dtype)


def _mix(x, ret, proj, conv_w, conv_b, w_ret_up, w_conv_up, w_out, g_ffn, wqt, *, seq, tt=256):
    t, d = x.shape
    w = ret.shape[1]
    nq = wqt.shape[0]
    halo = 16
    blk = proj.shape[1] // w
    cb_i, cc_i, ch_i, gr_i, gc_i = blk - 5, blk - 4, blk - 3, blk - 2, blk - 1

    def tile(col):
        return pl.BlockSpec((tt, w), lambda i: (i, col))

    def halo_spec(col):
        return pl.BlockSpec((halo, w), lambda i: (jnp.maximum(i * (tt // halo) - 1, 0), col))

    def const(shape):
        return pl.BlockSpec(shape, lambda i: (0,) * len(shape))

    return pl.pallas_call(
        functools.partial(_mix_kernel, tiles_per_seq=seq // tt),
        grid=(t // tt,),
        in_specs=[
            pl.BlockSpec((tt, d), lambda i: (i, 0)),
            pl.BlockSpec((tt, w), lambda i: (i, 0)),
            tile(cb_i), tile(cc_i), tile(ch_i), tile(gr_i), tile(gc_i),
            halo_spec(cc_i), halo_spec(ch_i),
            const((CONV_K, w)), const((1, w)),
            const((w, d)), const((w, d)), const((d, d)), const((1, d)), const((nq, d)),
        ],
        out_specs=[
            pl.BlockSpec((tt, d), lambda i: (i, 0)),
            pl.BlockSpec((d, tt), lambda i: (0, i)),
            pl.BlockSpec((nq, tt), lambda i: (0, i)),
        ],
        out_shape=[
            jax.ShapeDtypeStruct((t, d), F32),
            jax.ShapeDtypeStruct((d, t), BF16),
            jax.ShapeDtypeStruct((nq, t), BF16),
        ],
        compiler_params=_params("parallel"),
        name="mix",
    )(x, ret, proj, proj, proj, proj, proj, proj, proj, conv_w, conv_b, w_ret_up, w_conv_up, w_out, g_ffn, wqt)


def _candidate_tables():
    k = PEER_TOPK
    pos, valid = [], []
    for j in range(k):
        pos.append(j)
        valid.append(True)
    for i in range(1, 8):
        for j in range(8):
            pos.append(i * k + j)
            valid.append((i + 1) * (j + 1) <= k)
    for i in range(8, k):
        pos.append(i * k)
        valid.append(True)
    pos = np.asarray(pos, np.float32)
    neg = np.where(np.asarray(valid), 0.0, -np.inf).astype(np.float32)
    return (np.ascontiguousarray(np.broadcast_to(pos[:, None], (pos.size, LANES))),
            np.ascontiguousarray(np.broadcast_to(neg[:, None], (neg.size, LANES))))


def _top_ranks(s):
    k = PEER_TOPK
    nkeys = s.shape[0]
    key = lax.broadcasted_iota(jnp.int32, s.shape, 0)
    row = lax.broadcasted_iota(jnp.int32, (k, s.shape[1]), 0)
    rank = jnp.full(s.shape, float(k), F32)
    vals = jnp.zeros((k, s.shape[1]), F32)
    work = s
    for r in range(k):
        m = jnp.max(work, axis=0, keepdims=True)
        first = jnp.min(jnp.where(work == m, key, nkeys), axis=0, keepdims=True)
        sel = key == first
        rank = jnp.where(sel, float(r), rank)
        work = jnp.where(sel, -jnp.inf, work)
        vals = jnp.where(row == r, m, vals)
    return rank, vals


def _sort16_network():
    def merge(lo, hi, r):
        step = r * 2
        if step < hi - lo:
            yield from merge(lo, hi, step)
            yield from merge(lo + r, hi, step)
            yield from [(i, i + r) for i in range(lo + r, hi - r, step)]
        else:
            yield (lo, lo + r)

    def sort(lo, hi):
        if hi - lo >= 1:
            mid = lo + (hi - lo) // 2
            yield from sort(lo, mid)
            yield from sort(mid + 1, hi)
            yield from merge(lo, hi, 1)

    return tuple(sort(0, PEER_TOPK - 1))


def _sublane_allreduce(x, op):
    for shift in (4, 2, 1):
        x = op(x, pltpu.roll(x, shift, 0))
    return x


def _sorted_top_values(s):
    k = PEER_TOPK
    x = [s[8 * i:8 * i + 8, :] for i in range(s.shape[0] // 8)]
    assert len(x) == k
    for i, j in _sort16_network():
        x[i], x[j] = jnp.maximum(x[i], x[j]), jnp.minimum(x[i], x[j])
    for shift in (4, 2, 1):
        p = [pltpu.roll(v, shift, 0) for v in x]
        y = [jnp.maximum(x[r], p[k - 1 - r]) for r in range(k)]
        d = k // 2
        while d:
            for i in range(k):
                if not i & d:
                    y[i], y[i + d] = jnp.maximum(y[i], y[i + d]), jnp.minimum(y[i], y[i + d])
            d //= 2
        x = y
    return x


_SORT10_NETWORK = ((0, 5), (1, 6), (2, 7), (3, 8), (4, 9), (0, 3), (1, 4), (5, 8), (6, 9), (0, 2), (3, 6), (7, 9),
                   (0, 1), (2, 4), (5, 7), (8, 9), (1, 2), (3, 5), (4, 6), (7, 8), (1, 3), (2, 5), (4, 7), (6, 8),
                   (2, 3), (6, 7), (3, 4), (5, 6), (4, 5))


def _bitonic_sort(y):
    d = len(y) // 2
    while d:
        for i in range(len(y)):
            if not i & d:
                y[i], y[i + d] = jnp.maximum(y[i], y[i + d]), jnp.minimum(y[i], y[i + d])
        d //= 2
    return y


def _kth_largest(cand):
    k = PEER_TOPK
    x = list(cand)
    assert len(x) == 10
    for i, j in _SORT10_NETWORK:
        x[i], x[j] = jnp.maximum(x[i], x[j]), jnp.minimum(x[i], x[j])
    x = x + [None] * (k - len(x))
    for shift in (4, 2, 1):
        y = []
        for r in range(k):
            own, other = x[r], x[k - 1 - r]
            other = None if other is None else pltpu.roll(other, shift, 0)
            if own is None or other is None:
                y.append(own if other is None else other)
            else:
                y.append(jnp.maximum(own, other))
        assert all(v is not None for v in y)
        if shift == 1:
            tau = y[0]
            for v in y[1:]:
                tau = jnp.minimum(tau, v)
            return tau
        x = _bitonic_sort(y)


def _count_leading(b, pred):
    m8 = pred(b[7])
    m4 = pred(jnp.where(m8, b[11], b[3]))
    m2 = pred(jnp.where(m8, jnp.where(m4, b[13], b[9]), jnp.where(m4, b[5], b[1])))
    m1 = pred(jnp.where(m8, jnp.where(m4, jnp.where(m2, b[14], b[12]), jnp.where(m2, b[10], b[8])),
                        jnp.where(m4, jnp.where(m2, b[6], b[4]), jnp.where(m2, b[2], b[0]))))
    cnt = (jnp.where(m8, 8.0, 0.0) + jnp.where(m4, 4.0, 0.0)) + (jnp.where(m2, 2.0, 0.0) + jnp.where(m1, 1.0, 0.0))
    return jnp.where(pred(b[15]), 16.0, cnt)


def _select_fast(s1, s2):
    k = PEER_TOPK
    assert k == 16
    n = s1.shape[1]
    a = _sorted_top_values(s1)
    b = _sorted_top_values(s2)
    row = lax.broadcasted_iota(jnp.int32, (8, n), 0)

    def rows_of(vals):
        out = vals[0]
        for r in range(1, 8):
            out = jnp.where(row == r, vals[r], out)
        return out

    b_lo, b_hi, a_hi = rows_of(b[0:8]), rows_of(b[8:16]), rows_of(a[8:16])
    neg_inf = jnp.float32(-jnp.inf)
    cand = [a[0] + b_lo, a[0] + b_hi]
    for i in range(1, 8):
        cand.append(jnp.where(row < k // (i + 1), a[i] + b_lo, neg_inf))
    cand.append(a_hi + b[0])
    tau = _kth_largest(cand)
    top = a[0] + b[0]
    z = None
    for c in cand:
        term = jnp.where(c >= tau, jnp.exp(c - top), 0.0)
        z = term if z is None else z + term
    inv_z = 1.0 / _sublane_allreduce(z, jnp.add)

    rank2, count1, e1, e2 = [], [], [], []
    n_top2 = None
    n_pairs = None
    for i in range(s1.shape[0] // 8):
        x1 = s1[8 * i:8 * i + 8, :]
        x2 = s2[8 * i:8 * i + 8, :]
        r2 = _count_leading(b, lambda t: t > x2)
        c1 = _count_leading(b, lambda t: x1 + t >= tau)
        rank2.append(r2)
        count1.append(c1)
        e1.append(jnp.exp(x1 - a[0]) * inv_z)
        e2.append(jnp.exp(x2 - b[0]))
        in_top = jnp.where(r2 < float(k), 1.0, 0.0)
        n_top2 = in_top if n_top2 is None else n_top2 + in_top
        n_pairs = c1 if n_pairs is None else n_pairs + c1
    n_top2 = _sublane_allreduce(n_top2, jnp.add)
    n_pairs = _sublane_allreduce(n_pairs, jnp.add)
    suspicious = jnp.where((n_top2 != float(k)) | (n_pairs != float(k)), 1.0, 0.0)
    cat = lambda xs: jnp.concatenate(xs, axis=0)
    return cat(rank2), cat(count1), cat(e1), cat(e2), suspicious


def _select_exact(s1, s2, pos, neg):
    k = PEER_TOPK
    rank1, a = _top_ranks(s1)
    rank2, b = _top_ranks(s2)
    pieces = [a[0:1, :] + b]
    for i in range(1, 8):
        pieces.append(a[i:i + 1, :] + b[0:8, :])
    pieces.append(a[8:k, :] + b[0:1, :])
    cand = jnp.concatenate(pieces, axis=0) + neg
    work = cand
    chosen = jnp.zeros(cand.shape, F32)
    for r in range(k):
        m = jnp.max(work, axis=0, keepdims=True)
        first = jnp.min(jnp.where(work == m, pos, float(k * k)), axis=0, keepdims=True)
        sel = pos == first
        chosen = jnp.where(sel, 1.0, chosen)
        work = jnp.where(sel, -jnp.inf, work)
    top = a[0:1, :] + b[0:1, :]
    z = jnp.sum(jnp.where(chosen > 0.0, jnp.exp(cand - top), 0.0), axis=0, keepdims=True)
    row8 = lax.broadcasted_iota(jnp.int32, (8, LANES), 0)
    cnt_lo = jnp.where(row8 == 0, jnp.sum(chosen[0:k, :], axis=0, keepdims=True), 0.0)
    for i in range(1, 8):
        lo = k + 8 * (i - 1)
        cnt_lo = jnp.where(row8 == i, jnp.sum(chosen[lo:lo + 8, :], axis=0, keepdims=True), cnt_lo)
    cnt = jnp.concatenate([cnt_lo, chosen[k + 56:k + 64, :]], axis=0)
    count1 = jnp.zeros(s1.shape, F32)
    for r in range(k):
        count1 = jnp.where(rank1 == float(r), cnt[r:r + 1, :], count1)
    return rank2, count1, jnp.exp(s1 - a[0:1, :]) / z, jnp.exp(s2 - b[0:1, :])


def _select_kernel(qt_ref, k1_ref, k2_ref, pos_ref, neg_ref, r2_ref, e2_ref, c1_ref, e1_ref, s1_ref, s2_ref):
    half = PEER_NKEYS
    s1_ref[...] = _dot(k1_ref[0], qt_ref[0:half, :])
    s2_ref[...] = _dot(k2_ref[0], qt_ref[half:2 * half, :])

    groups_per_iter = 2

    def body(it, carry):
        groups = [it * groups_per_iter + i for i in range(groups_per_iter)]
        lanes = [pl.ds(pl.multiple_of(g * LANES, LANES), LANES) for g in groups]

        def emit(g, ln, rank2, count1, e1, e2):
            r2_ref[0, :, ln] = rank2.astype(r2_ref.dtype)
            c1_ref[0, g] = count1
            e1_ref[0, g] = e1
            e2_ref[0, :, ln] = e2.astype(e2_ref.dtype)

        fast = [_select_fast(s1_ref[:, ln], s2_ref[:, ln]) for ln in lanes]
        for g, ln, (*tables, _) in zip(groups, lanes, fast):
            emit(g, ln, *tables)
        for g, ln, (*_, suspicious) in zip(groups, lanes, fast):
            @pl.when(jnp.max(suspicious) > 0.0)
            def _():
                emit(g, ln, *_select_exact(s1_ref[:, ln], s2_ref[:, ln], pos_ref[...], neg_ref[...]))

        return carry

    lax.fori_loop(0, s1_ref.shape[1] // (LANES * groups_per_iter), body, 0)


def _select(qt, k1, k2, *, ts=1024):
    nq, t = qt.shape
    h = PEER_HEADS
    nk = PEER_NKEYS
    pos, neg = _candidate_tables()
    out_spec = pl.BlockSpec((1, nk, ts), lambda i, j: (j, 0, i))
    row_spec = pl.BlockSpec((1, ts // LANES, nk, LANES), lambda i, j: (j, i, 0, 0))
    return pl.pallas_call(
        _select_kernel,
        grid=(t // ts, h),
        in_specs=[
            pl.BlockSpec((nq // h, ts), lambda i, j: (j, i)),
            pl.BlockSpec((1, nk, nk), lambda i, j: (j, 0, 0)),
            pl.BlockSpec((1, nk, nk), lambda i, j: (j, 0, 0)),
            pl.BlockSpec(pos.shape, lambda i, j: (0, 0)),
            pl.BlockSpec(neg.shape, lambda i, j: (0, 0)),
        ],
        out_specs=[out_spec, out_spec, row_spec, row_spec],
        out_shape=[
            jax.ShapeDtypeStruct((h, nk, t), BF16),
            jax.ShapeDtypeStruct((h, nk, t), BF16),
            jax.ShapeDtypeStruct((h, t // LANES, nk, LANES), F32),
            jax.ShapeDtypeStruct((h, t // LANES, nk, LANES), F32),
        ],
        scratch_shapes=[pltpu.VMEM((nk, ts), F32), pltpu.VMEM((nk, ts), F32)],
        compiler_params=_params("parallel", "parallel"),
        name="peer_select",
    )(qt, k1, k2, jnp.asarray(pos), jnp.asarray(neg))


def _dense_kernel(a2t_ref, u_ref, vt_ref, r2_ref, e2_ref, c1_ref, e1_ref, h1_ref, o_ref, acc_ref, g_ref, act_ref, *, keys_per_block):
    j = pl.program_id(1)
    nk = PEER_NKEYS
    tt = a2t_ref.shape[1]

    @pl.when(j == 0)
    def _():
        acc_ref[...] = jnp.zeros_like(acc_ref)

    act_ref[...] = _dot(u_ref[...], a2t_ref[...]).astype(act_ref.dtype)
    pack = 16
    for kk in range(keys_per_block):
        i1 = j * keys_per_block + kk
        rows = slice(kk * nk, (kk + 1) * nk)

        def key_row(ref, h):
            parts = [jnp.broadcast_to(ref[h, g, pl.ds(i1, 1), :], (pack, LANES)) for g in range(tt // LANES)]
            row = jnp.concatenate(parts, axis=1).astype(BF16)
            return jnp.concatenate([row] * (nk // pack), axis=0)

        w = None
        for h in range(PEER_HEADS):
            term = jnp.where(r2_ref[h] < key_row(c1_ref, h), e2_ref[h], jnp.zeros((), BF16)) * key_row(e1_ref, h)
            w = term if w is None else w + term
        a = act_ref[rows, :]
        half = jnp.asarray(0.5, a.dtype)
        gelu = (half * a) * (jnp.asarray(1.0, a.dtype) + lax.erf(a * jnp.asarray(2.0 ** -0.5, a.dtype)))
        g_ref[rows, :] = w * gelu
    acc_ref[...] += _dot(vt_ref[...], g_ref[...])

    @pl.when(j == pl.num_programs(1) - 1)
    def _():
        o_ref[...] = h1_ref[...] + acc_ref[...].T


def _dense(a2t, u, vt, r2, e2, c1, e1, h1, *, tt=512, te=1024):
    d, t = a2t.shape
    ne = u.shape[0]
    h, nk = PEER_HEADS, PEER_NKEYS
    tab = pl.BlockSpec((h, nk, tt), lambda i, j: (0, 0, i))
    rowtab = pl.BlockSpec((h, tt // LANES, nk, LANES), lambda i, j: (0, i, 0, 0))
    return pl.pallas_call(
        functools.partial(_dense_kernel, keys_per_block=te // nk),
        grid=(t // tt, ne // te),
        in_specs=[
            pl.BlockSpec((d, tt), lambda i, j: (0, i)),
            pl.BlockSpec((te, d), lambda i, j: (j, 0)),
            pl.BlockSpec((d, te), lambda i, j: (0, j)),
            tab, tab, rowtab, rowtab,
            pl.BlockSpec((tt, d), lambda i, j: (i, 0)),
        ],
        out_specs=pl.BlockSpec((tt, d), lambda i, j: (i, 0)),
        out_shape=jax.ShapeDtypeStruct((t, d), F32),
        scratch_shapes=[pltpu.VMEM((d, tt), F32), pltpu.VMEM((te, tt), BF16), pltpu.VMEM((te, tt), BF16)],
        compiler_params=_params("parallel", "arbitrary"),
        name="peer_dense",
    )(a2t, u, vt, r2, e2, c1, e1, h1)


def _final_kernel(h_ref, p_ref, gple_ref, wproj_ref, wgate_ref, gfin_ref, o_ref, *, last):
    h2 = h_ref[...]
    ple = _dot(p_ref[...].astype(BF16), wproj_ref[...])
    gate = jax.nn.sigmoid(_dot(_rms(h2, gple_ref[...]).astype(BF16), wgate_ref[...]))
    h3 = h2 + gate * ple
    o_ref[...] = _rms(h3, gfin_ref[...]) if last else h3


def _final(h2, p, g_ple, w_proj, w_gate, g_final, *, last, tt=512):
    t, d = h2.shape
    pd = p.shape[1]
    return pl.pallas_call(
        functools.partial(_final_kernel, last=last),
        grid=(t // tt,),
        in_specs=[
            pl.BlockSpec((tt, d), lambda i: (i, 0)),
            pl.BlockSpec((tt, pd), lambda i: (i, 0)),
            pl.BlockSpec((1, d), lambda i: (0, 0)),
            pl.BlockSpec((pd, d), lambda i: (0, 0)),
            pl.BlockSpec((d, d), lambda i: (0, 0)),
            pl.BlockSpec((1, d), lambda i: (0, 0)),
        ],
        out_specs=pl.BlockSpec((tt, d), lambda i: (i, 0)),
        out_shape=jax.ShapeDtypeStruct((t, d), F32),
        compiler_params=_params("parallel"),
        name="ple_final",
    )(h2, p, g_ple, w_proj, w_gate, g_final)


def kernel(x, p, g_mix, w_in, conv_w, conv_b, w_ret_up, w_conv_up, w_out, g_ffn, w_peer_q, peer_k1, peer_k2, peer_u, peer_v, g_ple, w_ple_proj, w_ple_gate, g_final):
    b, s, d = x.shape
    t = b * s
    depth = w_in.shape[0]
    tables = _retention_tables(s)
    h = x.reshape(t, d)
    for i in range(depth):
        proj = _inproj(h, g_mix[i][None], w_in[i].astype(BF16))
        ret = _retention(proj, *tables, batch=b, seq=s)
        h1, a2t, qt = _mix(h, ret, proj, conv_w[i], conv_b[i][None], w_ret_up[i].astype(BF16),
                           w_conv_up[i].astype(BF16), w_out[i].astype(BF16), g_ffn[i][None],
                           w_peer_q[i].T.astype(BF16), seq=s)
        r2, e2, c1, e1 = _select(qt, peer_k1[i].astype(BF16), peer_k2[i].astype(BF16))
        h2 = _dense(a2t, peer_u[i].astype(BF16), peer_v[i].T.astype(BF16), r2, e2, c1, e1, h1)
        h = _final(h2, p[i].reshape(t, -1), g_ple[i][None], w_ple_proj[i].astype(BF16),
                   w_ple_gate[i].astype(BF16), g_final[None], last=i == depth - 1)
    return h.reshape(b, s, d)
```

```python
import functools

import numpy as np
import jax
import jax.numpy as jnp
from jax import lax
from jax.experimental import pallas as pl
from jax.experimental.pallas import tpu as pltpu

F32 = jnp.float32
BF16 = jnp.bfloat16
EPS = 1e-6

RET_HEADS = 4
RET_DK = 128
RET_DV = 256
RET_CHUNK = 128
ROPE_BASE = 10000.0
CONV_K = 3
PEER_HEADS = 8
PEER_NKEYS = 128
PEER_TOPK = 16

VMEM_LIMIT_BYTES = 56 * 1024 * 1024
LANES = 128


def _params(*sem):
    return pltpu.CompilerParams(dimension_semantics=sem, vmem_limit_bytes=VMEM_LIMIT_BYTES)


def _rms(x, g):
    return x * lax.rsqrt(jnp.mean(x * x, axis=-1, keepdims=True) + EPS) * g


def _dot(a, b):
    return jnp.dot(a, b, preferred_element_type=F32)


def _inproj_kernel(x_ref, g_ref, w_ref, o_ref):
    a = _rms(x_ref[...], g_ref[...]).astype(BF16)
    o_ref[...] = _dot(a, w_ref[...]).astype(o_ref.dtype)


def _inproj(x, g, w, *, tt=512):
    t, d = x.shape
    n = w.shape[1]
    return pl.pallas_call(
        _inproj_kernel,
        grid=(t // tt,),
        in_specs=[
            pl.BlockSpec((tt, d), lambda i: (i, 0)),
            pl.BlockSpec((1, d), lambda i: (0, 0)),
            pl.BlockSpec((d, n), lambda i: (0, 0), pipeline_mode=pl.Buffered(1)),
        ],
        out_specs=pl.BlockSpec((tt, n), lambda i: (i, 0)),
        out_shape=jax.ShapeDtypeStruct((t, n), BF16),
        compiler_params=_params("parallel"),
        name="inproj",
    )(x, g, w)


def _ret_kernel(q_ref, k_ref, v_ref, g_ref, cos_ref, sin_ref, dec_ref, xi_ref, zeta_ref, o_ref, st_ref, *, chunk):
    @pl.when(pl.program_id(1) == 0)
    def _():
        st_ref[...] = jnp.zeros_like(st_ref)

    scale = RET_DK ** -0.5
    half = RET_DK // 2

    def body(c, carry):
        r = pl.ds(pl.multiple_of(c * chunk, chunk), chunk)
        cosv = cos_ref[r, :]
        sinv = sin_ref[r, :]
        for h in range(RET_HEADS):
            qk = slice(h * RET_DK, (h + 1) * RET_DK)
            vv = slice(h * RET_DV, (h + 1) * RET_DV)
            q = q_ref[r, qk].astype(F32)
            k = k_ref[r, qk].astype(F32)
            q = q * cosv + pltpu.roll(q, half, 1) * sinv
            k = (k * cosv + pltpu.roll(k, half, 1) * sinv) * scale
            qb = q.astype(BF16)
            kb = k.astype(BF16)
            v = v_ref[r, vv]
            s = lax.dot_general(qb, kb, (((1,), (1,)), ((), ())), preferred_element_type=F32) * dec_ref[h]
            intra = _dot(s.astype(BF16), v)
            st = st_ref[h]
            cross = _dot(qb, st.astype(BF16)) * xi_ref[h]
            vz = (v.astype(F32) * zeta_ref[h]).astype(BF16)
            kv = lax.dot_general(kb, vz, (((0,), (0,)), ((), ())), preferred_element_type=F32)
            st_ref[h] = st * xi_ref[h, chunk - 1:chunk, :] + kv
            y = intra + cross
            yc = y - jnp.mean(y, axis=-1, keepdims=True)
            yn = yc * lax.rsqrt(jnp.mean(yc * yc, axis=-1, keepdims=True) + EPS)
            g = g_ref[r, vv].astype(F32)
            o_ref[r, vv] = (g * jax.nn.sigmoid(g) * yn).astype(o_ref.dtype)
        return carry

    lax.fori_loop(0, q_ref.shape[0] // chunk, body, 0)


def _retention(proj, cosf, sinf, dec, xi, zeta, *, batch, seq, sb=1024):
    t = proj.shape[0]
    h, c = RET_HEADS, RET_CHUNK
    sb = min(sb, seq)
    nsb = seq // sb
    qk_w, v_w = h * RET_DK, h * RET_DV
    assert 2 * qk_w == v_w

    def rows(col):
        return lambda b, s: (b * nsb + s, col)

    def const(shape):
        return pl.BlockSpec(shape, lambda b, s: (0,) * len(shape))

    return pl.pallas_call(
        functools.partial(_ret_kernel, chunk=c),
        grid=(batch, nsb),
        in_specs=[
            pl.BlockSpec((sb, qk_w), rows(0)),
            pl.BlockSpec((sb, qk_w), rows(1)),
            pl.BlockSpec((sb, v_w), rows(1)),
            pl.BlockSpec((sb, v_w), rows(2)),
            pl.BlockSpec((sb, RET_DK), lambda b, s: (s, 0)),
            pl.BlockSpec((sb, RET_DK), lambda b, s: (s, 0)),
            const((h, c, c)), const((h, c, RET_DV)), const((h, c, RET_DV)),
        ],
        out_specs=pl.BlockSpec((sb, v_w), lambda b, s: (b * nsb + s, 0)),
        out_shape=jax.ShapeDtypeStruct((t, v_w), BF16),
        scratch_shapes=[pltpu.VMEM((h, RET_DK, RET_DV), F32)],
        compiler_params=_params("parallel", "arbitrary"),
        name="retention",
    )(proj, proj, proj, proj, cosf, sinf, dec, xi, zeta)


def _retention_tables(seq):
    h, c = RET_HEADS, RET_CHUNK
    half = RET_DK // 2
    inv = ROPE_BASE ** (-jnp.arange(half, dtype=F32) / half)
    ang = jnp.arange(seq, dtype=F32)[:, None] * inv[None, :]
    cos, sin = jnp.cos(ang), jnp.sin(ang)
    cosf = jnp.concatenate([cos, cos], axis=-1)
    sinf = jnp.concatenate([-sin, sin], axis=-1)
    log_gamma = jnp.log1p(-(2.0 ** (-5.0 - jnp.arange(h, dtype=F32))))
    pos = jnp.arange(c, dtype=F32)
    diff = pos[:, None] - pos[None, :]
    causal = diff >= 0
    dec = jnp.where(causal[None], jnp.exp(log_gamma[:, None, None] * jnp.where(causal, diff, 0.0)[None]), 0.0)
    xi = jnp.exp(log_gamma[:, None] * (pos[None, :] + 1.0))
    zeta = jnp.exp(log_gamma[:, None] * (c - 1.0 - pos[None, :]))
    xi = jnp.broadcast_to(xi[:, :, None], (h, c, RET_DV))
    zeta = jnp.broadcast_to(zeta[:, :, None], (h, c, RET_DV))
    return cosf, sinf, dec, xi, zeta


def _mix_kernel(x_ref, ret_ref, cb_ref, cc_ref, ch_ref, gr_ref, gc_ref, cch_ref, chh_ref, cw_ref, cbias_ref,
                wru_ref, wcu_ref, wo_ref, gffn_ref, wqt_ref, h1_ref, a2t_ref, qt_ref, *, tiles_per_seq):
    i = pl.program_id(0)
    u = cc_ref[...].astype(F32) * ch_ref[...].astype(F32)
    halo = cch_ref[...].astype(F32) * chh_ref[...].astype(F32)
    halo = jnp.where(i % tiles_per_seq == 0, 0.0, halo)
    n_halo = halo.shape[0]
    p1 = halo[n_halo - 1:n_halo, :]
    p2 = halo[n_halo - 2:n_halo - 1, :]
    row = lax.broadcasted_iota(jnp.int32, u.shape, 0)
    u1 = jnp.where(row == 0, p1, pltpu.roll(u, 1, 0))
    u2 = jnp.where(row == 0, p2, jnp.where(row == 1, p1, pltpu.roll(u, 2, 0)))
    cw = cw_ref[...]
    z = cw[0:1, :] * u2 + cw[1:2, :] * u1 + cw[2:3, :] * u + cbias_ref[...]
    y_conv = _dot((cb_ref[...].astype(F32) * z).astype(BF16), wcu_ref[...])
    y_ret = _dot(ret_ref[...], wru_ref[...])
    merged = (jax.nn.sigmoid(gr_ref[...].astype(F32)) * y_ret
              + jax.nn.sigmoid(gc_ref[...].astype(F32)) * y_conv)
    h1 = x_ref[...] + _dot(merged.astype(BF16), wo_ref[...])
    h1_ref[...] = h1
    a2t = _rms(h1, gffn_ref[...]).T.astype(BF16)
    a2t_ref[...] = a2t
    qt_ref[...] = _dot(wqt_ref[...], a2t).astype(qt_ref.dtype)


def _mix(x, ret, proj, conv_w, conv_b, w_ret_up, w_conv_up, w_out, g_ffn, wqt, *, seq, tt=512):
    t, d = x.shape
    w = ret.shape[1]
    nq = wqt.shape[0]
    halo = 16
    blk = proj.shape[1] // w
    cb_i, cc_i, ch_i, gr_i, gc_i = blk - 5, blk - 4, blk - 3, blk - 2, blk - 1

    def tile(col):
        return pl.BlockSpec((tt, w), lambda i: (i, col))

    def halo_spec(col):
        return pl.BlockSpec((halo, w), lambda i: (jnp.maximum(i * (tt // halo) - 1, 0), col))

    def const(shape):
        return pl.BlockSpec(shape, lambda i: (0,) * len(shape), pipeline_mode=pl.Buffered(1))

    return pl.pallas_call(
        functools.partial(_mix_kernel, tiles_per_seq=seq // tt),
        grid=(t // tt,),
        in_specs=[
            pl.BlockSpec((tt, d), lambda i: (i, 0)),
            pl.BlockSpec((tt, w), lambda i: (i, 0)),
            tile(cb_i), tile(cc_i), tile(ch_i), tile(gr_i), tile(gc_i),
            halo_spec(cc_i), halo_spec(ch_i),
            const((CONV_K, w)), const((1, w)),
            const((w, d)), const((w, d)), const((d, d)), const((1, d)), const((nq, d)),
        ],
        out_specs=[
            pl.BlockSpec((tt, d), lambda i: (i, 0)),
            pl.BlockSpec((d, tt), lambda i: (0, i)),
            pl.BlockSpec((nq, tt), lambda i: (0, i)),
        ],
        out_shape=[
            jax.ShapeDtypeStruct((t, d), F32),
            jax.ShapeDtypeStruct((d, t), BF16),
            jax.ShapeDtypeStruct((nq, t), BF16),
        ],
        compiler_params=_params("parallel"),
        name="mix",
    )(x, ret, proj, proj, proj, proj, proj, proj, proj, conv_w, conv_b, w_ret_up, w_conv_up, w_out, g_ffn, wqt)


def _candidate_tables():
    k = PEER_TOPK
    pos, valid = [], []
    for j in range(k):
        pos.append(j)
        valid.append(True)
    for i in range(1, 8):
        for j in range(8):
            pos.append(i * k + j)
            valid.append((i + 1) * (j + 1) <= k)
    for i in range(8, k):
        pos.append(i * k)
        valid.append(True)
    pos = np.asarray(pos, np.float32)
    neg = np.where(np.asarray(valid), 0.0, -np.inf).astype(np.float32)
    return (np.ascontiguousarray(np.broadcast_to(pos[:, None], (pos.size, LANES))),
            np.ascontiguousarray(np.broadcast_to(neg[:, None], (neg.size, LANES))))


def _top_ranks(s):
    k = PEER_TOPK
    nkeys = s.shape[0]
    key = lax.broadcasted_iota(jnp.int32, s.shape, 0)
    row = lax.broadcasted_iota(jnp.int32, (k, s.shape[1]), 0)
    rank = jnp.full(s.shape, float(k), F32)
    vals = jnp.zeros((k, s.shape[1]), F32)
    work = s
    for r in range(k):
        m = jnp.max(work, axis=0, keepdims=True)
        first = jnp.min(jnp.where(work == m, key, nkeys), axis=0, keepdims=True)
        sel = key == first
        rank = jnp.where(sel, float(r), rank)
        work = jnp.where(sel, -jnp.inf, work)
        vals = jnp.where(row == r, m, vals)
    return rank, vals


def _sort16_network():
    def merge(lo, hi, r):
        step = r * 2
        if step < hi - lo:
            yield from merge(lo, hi, step)
            yield from merge(lo + r, hi, step)
            yield from [(i, i + r) for i in range(lo + r, hi - r, step)]
        else:
            yield (lo, lo + r)

    def sort(lo, hi):
        if hi - lo >= 1:
            mid = lo + (hi - lo) // 2
            yield from sort(lo, mid)
            yield from sort(mid + 1, hi)
            yield from merge(lo, hi, 1)

    return tuple(sort(0, PEER_TOPK - 1))


def _sublane_allreduce(x, op):
    for shift in (4, 2, 1):
        x = op(x, pltpu.roll(x, shift, 0))
    return x


def _sorted_top_values(s):
    k = PEER_TOPK
    x = [s[8 * i:8 * i + 8, :] for i in range(s.shape[0] // 8)]
    assert len(x) == k
    for i, j in _sort16_network():
        x[i], x[j] = jnp.maximum(x[i], x[j]), jnp.minimum(x[i], x[j])
    for shift in (4, 2, 1):
        p = [pltpu.roll(v, shift, 0) for v in x]
        y = [jnp.maximum(x[r], p[k - 1 - r]) for r in range(k)]
        d = k // 2
        while d:
            for i in range(k):
                if not i & d:
                    y[i], y[i + d] = jnp.maximum(y[i], y[i + d]), jnp.minimum(y[i], y[i + d])
            d //= 2
        x = y
    return x


_SORT10_NETWORK = ((0, 5), (1, 6), (2, 7), (3, 8), (4, 9), (0, 3), (1, 4), (5, 8), (6, 9), (0, 2), (3, 6), (7, 9),
                   (0, 1), (2, 4), (5, 7), (8, 9), (1, 2), (3, 5), (4, 6), (7, 8), (1, 3), (2, 5), (4, 7), (6, 8),
                   (2, 3), (6, 7), (3, 4), (5, 6), (4, 5))


def _bitonic_sort(y):
    d = len(y) // 2
    while d:
        for i in range(len(y)):
            if not i & d:
                y[i], y[i + d] = jnp.maximum(y[i], y[i + d]), jnp.minimum(y[i], y[i + d])
        d //= 2
    return y


def _kth_largest(cand):
    k = PEER_TOPK
    x = list(cand)
    assert len(x) == 10
    for i, j in _SORT10_NETWORK:
        x[i], x[j] = jnp.maximum(x[i], x[j]), jnp.minimum(x[i], x[j])
    x = x + [None] * (k - len(x))
    for shift in (4, 2, 1):
        y = []
        for r in range(k):
            own, other = x[r], x[k - 1 - r]
            other = None if other is None else pltpu.roll(other, shift, 0)
            if own is None or other is None:
                y.append(own if other is None else other)
            else:
                y.append(jnp.maximum(own, other))
        assert all(v is not None for v in y)
        if shift == 1:
            tau = y[0]
            for v in y[1:]:
                tau = jnp.minimum(tau, v)
            return tau
        x = _bitonic_sort(y)


def _count_leading(b, pred):
    m8 = pred(b[7])
    m4 = pred(jnp.where(m8, b[11], b[3]))
    m2 = pred(jnp.where(m8, jnp.where(m4, b[13], b[9]), jnp.where(m4, b[5], b[1])))
    m1 = pred(jnp.where(m8, jnp.where(m4, jnp.where(m2, b[14], b[12]), jnp.where(m2, b[10], b[8])),
                        jnp.where(m4, jnp.where(m2, b[6], b[4]), jnp.where(m2, b[2], b[0]))))
    cnt = (jnp.where(m8, 8.0, 0.0) + jnp.where(m4, 4.0, 0.0)) + (jnp.where(m2, 2.0, 0.0) + jnp.where(m1, 1.0, 0.0))
    return jnp.where(pred(b[15]), 16.0, cnt)


def _select_fast(s1, s2):
    k = PEER_TOPK
    assert k == 16
    n = s1.shape[1]
    a = _sorted_top_values(s1)
    b = _sorted_top_values(s2)
    row = lax.broadcasted_iota(jnp.int32, (8, n), 0)

    def rows_of(vals):
        out = vals[0]
        for r in range(1, 8):
            out = jnp.where(row == r, vals[r], out)
        return out

    b_lo, b_hi, a_hi = rows_of(b[0:8]), rows_of(b[8:16]), rows_of(a[8:16])
    neg_inf = jnp.float32(-jnp.inf)
    cand = [a[0] + b_lo, a[0] + b_hi]
    for i in range(1, 8):
        cand.append(jnp.where(row < k // (i + 1), a[i] + b_lo, neg_inf))
    cand.append(a_hi + b[0])
    tau = _kth_largest(cand)
    top = a[0] + b[0]
    z = None
    for c in cand:
        term = jnp.where(c >= tau, jnp.exp(c - top), 0.0)
        z = term if z is None else z + term
    inv_z = 1.0 / _sublane_allreduce(z, jnp.add)

    rank2, count1, e1, e2 = [], [], [], []
    n_top2 = None
    n_pairs = None
    for i in range(s1.shape[0] // 8):
        x1 = s1[8 * i:8 * i + 8, :]
        x2 = s2[8 * i:8 * i + 8, :]
        r2 = _count_leading(b, lambda t: t > x2)
        c1 = _count_leading(b, lambda t: x1 + t >= tau)
        rank2.append(r2)
        count1.append(c1)
        e1.append(jnp.exp(x1 - a[0]) * inv_z)
        e2.append(jnp.exp(x2 - b[0]))
        in_top = jnp.where(r2 < float(k), 1.0, 0.0)
        n_top2 = in_top if n_top2 is None else n_top2 + in_top
        n_pairs = c1 if n_pairs is None else n_pairs + c1
    n_top2 = _sublane_allreduce(n_top2, jnp.add)
    n_pairs = _sublane_allreduce(n_pairs, jnp.add)
    suspicious = jnp.where((n_top2 != float(k)) | (n_pairs != float(k)), 1.0, 0.0)
    cat = lambda xs: jnp.concatenate(xs, axis=0)
    return cat(rank2), cat(count1), cat(e1), cat(e2), suspicious


def _select_exact(s1, s2, pos, neg):
    k = PEER_TOPK
    rank1, a = _top_ranks(s1)
    rank2, b = _top_ranks(s2)
    pieces = [a[0:1, :] + b]
    for i in range(1, 8):
        pieces.append(a[i:i + 1, :] + b[0:8, :])
    pieces.append(a[8:k, :] + b[0:1, :])
    cand = jnp.concatenate(pieces, axis=0) + neg
    work = cand
    chosen = jnp.zeros(cand.shape, F32)
    for r in range(k):
        m = jnp.max(work, axis=0, keepdims=True)
        first = jnp.min(jnp.where(work == m, pos, float(k * k)), axis=0, keepdims=True)
        sel = pos == first
        chosen = jnp.where(sel, 1.0, chosen)
        work = jnp.where(sel, -jnp.inf, work)
    top = a[0:1, :] + b[0:1, :]
    z = jnp.sum(jnp.where(chosen > 0.0, jnp.exp(cand - top), 0.0), axis=0, keepdims=True)
    row8 = lax.broadcasted_iota(jnp.int32, (8, LANES), 0)
    cnt_lo = jnp.where(row8 == 0, jnp.sum(chosen[0:k, :], axis=0, keepdims=True), 0.0)
    for i in range(1, 8):
        lo = k + 8 * (i - 1)
        cnt_lo = jnp.where(row8 == i, jnp.sum(chosen[lo:lo + 8, :], axis=0, keepdims=True), cnt_lo)
    cnt = jnp.concatenate([cnt_lo, chosen[k + 56:k + 64, :]], axis=0)
    count1 = jnp.zeros(s1.shape, F32)
    for r in range(k):
        count1 = jnp.where(rank1 == float(r), cnt[r:r + 1, :], count1)
    return rank2, count1, jnp.exp(s1 - a[0:1, :]) / z, jnp.exp(s2 - b[0:1, :])


def _select_kernel(qt_ref, k1_ref, k2_ref, pos_ref, neg_ref, r2_ref, e2_ref, c1_ref, e1_ref, s1_ref, s2_ref):
    half = PEER_NKEYS
    s1_ref[...] = _dot(k1_ref[0], qt_ref[0:half, :])
    s2_ref[...] = _dot(k2_ref[0], qt_ref[half:2 * half, :])

    groups_per_iter = 2

    def body(it, carry):
        groups = [it * groups_per_iter + i for i in range(groups_per_iter)]
        lanes = [pl.ds(pl.multiple_of(g * LANES, LANES), LANES) for g in groups]

        def emit(g, ln, rank2, count1, e1, e2):
            r2_ref[0, :, ln] = rank2.astype(r2_ref.dtype)
            c1_ref[0, g] = count1
            e1_ref[0, g] = e1
            e2_ref[0, :, ln] = e2.astype(e2_ref.dtype)

        fast = [_select_fast(s1_ref[:, ln], s2_ref[:, ln]) for ln in lanes]
        for g, ln, (*tables, _) in zip(groups, lanes, fast):
            emit(g, ln, *tables)
        for g, ln, (*_, suspicious) in zip(groups, lanes, fast):
            @pl.when(jnp.max(suspicious) > 0.0)
            def _():
                emit(g, ln, *_select_exact(s1_ref[:, ln], s2_ref[:, ln], pos_ref[...], neg_ref[...]))

        return carry

    lax.fori_loop(0, s1_ref.shape[1] // (LANES * groups_per_iter), body, 0)


def _select(qt, k1, k2, *, ts=1024):
    nq, t = qt.shape
    h = PEER_HEADS
    nk = PEER_NKEYS
    pos, neg = _candidate_tables()
    out_spec = pl.BlockSpec((1, nk, ts), lambda i, j: (j, 0, i))
    row_spec = pl.BlockSpec((1, ts // LANES, nk, LANES), lambda i, j: (j, i, 0, 0))
    return pl.pallas_call(
        _select_kernel,
        grid=(t // ts, h),
        in_specs=[
            pl.BlockSpec((nq // h, ts), lambda i, j: (j, i)),
            pl.BlockSpec((1, nk, nk), lambda i, j: (j, 0, 0)),
            pl.BlockSpec((1, nk, nk), lambda i, j: (j, 0, 0)),
            pl.BlockSpec(pos.shape, lambda i, j: (0, 0)),
            pl.BlockSpec(neg.shape, lambda i, j: (0, 0)),
        ],
        out_specs=[out_spec, out_spec, row_spec, row_spec],
        out_shape=[
            jax.ShapeDtypeStruct((h, nk, t), BF16),
            jax.ShapeDtypeStruct((h, nk, t), BF16),
            jax.ShapeDtypeStruct((h, t // LANES, nk, LANES), F32),
            jax.ShapeDtypeStruct((h, t // LANES, nk, LANES), F32),
        ],
        scratch_shapes=[pltpu.VMEM((nk, ts), F32), pltpu.VMEM((nk, ts), F32)],
        compiler_params=_params("parallel", "parallel"),
        name="peer_select",
    )(qt, k1, k2, jnp.asarray(pos), jnp.asarray(neg))


def _dense_kernel(a2t_ref, u0_ref, u_ref, vt_ref, r2_ref, e2_ref, c1_ref, e1_ref, h1_ref, o_ref, acc_ref,
                  act_even_ref, act_odd_ref, *, keys_per_block):
    j = pl.program_id(1)
    nk = PEER_NKEYS
    tt = a2t_ref.shape[1]
    pack = 16

    @pl.when(j == 0)
    def _():
        acc_ref[...] = jnp.zeros_like(acc_ref)
        act_even_ref[...] = _dot(u0_ref[...], a2t_ref[...]).astype(act_even_ref.dtype)

    def step(act_ref, next_act_ref):
        gated = []
        for kk in range(keys_per_block):
            i1 = j * keys_per_block + kk

            def key_row(ref, h):
                parts = [jnp.broadcast_to(ref[h, g, pl.ds(i1, 1), :], (pack, LANES)) for g in range(tt // LANES)]
                row = jnp.concatenate(parts, axis=1).astype(BF16)
                return jnp.concatenate([row] * (nk // pack), axis=0)

            w = None
            for h in range(PEER_HEADS):
                term = jnp.where(r2_ref[h] < key_row(c1_ref, h), e2_ref[h], jnp.zeros((), BF16)) * key_row(e1_ref, h)
                w = term if w is None else w + term
            a = act_ref[kk * nk:(kk + 1) * nk, :]
            half = jnp.asarray(0.5, a.dtype)
            gelu = (half * a) * (jnp.asarray(1.0, a.dtype) + lax.erf(a * jnp.asarray(2.0 ** -0.5, a.dtype)))
            gated.append(w * gelu)
        acc_ref[...] += _dot(vt_ref[...], jnp.concatenate(gated, axis=0))
        next_act_ref[...] = _dot(u_ref[...], a2t_ref[...]).astype(next_act_ref.dtype)

    @pl.when(j % 2 == 0)
    def _():
        step(act_even_ref, act_odd_ref)

    @pl.when(j % 2 == 1)
    def _():
        step(act_odd_ref, act_even_ref)

    @pl.when(j == pl.num_programs(1) - 1)
    def _():
        o_ref[...] = h1_ref[...] + acc_ref[...].T


def _dense(a2t, u, vt, r2, e2, c1, e1, h1, *, tt=512, te=1024):
    d, t = a2t.shape
    ne = u.shape[0]
    h, nk = PEER_HEADS, PEER_NKEYS
    nb = ne // te
    tab = pl.BlockSpec((h, nk, tt), lambda i, j: (0, 0, i))
    rowtab = pl.BlockSpec((h, tt // LANES, nk, LANES), lambda i, j: (0, i, 0, 0))
    return pl.pallas_call(
        functools.partial(_dense_kernel, keys_per_block=te // nk),
        grid=(t // tt, nb),
        in_specs=[
            pl.BlockSpec((d, tt), lambda i, j: (0, i)),
            pl.BlockSpec((te, d), lambda i, j: (0, 0)),
            pl.BlockSpec((te, d), lambda i, j: (jnp.minimum(j + 1, nb - 1), 0)),
            pl.BlockSpec((d, te), lambda i, j: (0, j)),
            tab, tab, rowtab, rowtab,
            pl.BlockSpec((tt, d), lambda i, j: (i, 0)),
        ],
        out_specs=pl.BlockSpec((tt, d), lambda i, j: (i, 0)),
        out_shape=jax.ShapeDtypeStruct((t, d), F32),
        scratch_shapes=[pltpu.VMEM((d, tt), F32), pltpu.VMEM((te, tt), BF16), pltpu.VMEM((te, tt), BF16)],
        compiler_params=_params("parallel", "arbitrary"),
        name="peer_dense",
    )(a2t, u, u, vt, r2, e2, c1, e1, h1)


def _final_kernel(h_ref, p_ref, gple_ref, wproj_ref, wgate_ref, gfin_ref, o_ref, *, last):
    h2 = h_ref[...]
    ple = _dot(p_ref[...].astype(BF16), wproj_ref[...])
    gate = jax.nn.sigmoid(_dot(_rms(h2, gple_ref[...]).astype(BF16), wgate_ref[...]))
    h3 = h2 + gate * ple
    o_ref[...] = _rms(h3, gfin_ref[...]) if last else h3


def _final(h2, p, g_ple, w_proj, w_gate, g_final, *, last, tt=512):
    t, d = h2.shape
    pd = p.shape[1]
    return pl.pallas_call(
        functools.partial(_final_kernel, last=last),
        grid=(t // tt,),
        in_specs=[
            pl.BlockSpec((tt, d), lambda i: (i, 0)),
            pl.BlockSpec((tt, pd), lambda i: (i, 0)),
            pl.BlockSpec((1, d), lambda i: (0, 0)),
            pl.BlockSpec((pd, d), lambda i: (0, 0)),
            pl.BlockSpec((d, d), lambda i: (0, 0)),
            pl.BlockSpec((1, d), lambda i: (0, 0)),
        ],
        out_specs=pl.BlockSpec((tt, d), lambda i: (i, 0)),
        out_shape=jax.ShapeDtypeStruct((t, d), F32),
        compiler_params=_params("parallel"),
        name="ple_final",
    )(h2, p, g_ple, w_proj, w_gate, g_final)


def kernel(x, p, g_mix, w_in, conv_w, conv_b, w_ret_up, w_conv_up, w_out, g_ffn, w_peer_q, peer_k1, peer_k2, peer_u, peer_v, g_ple, w_ple_proj, w_ple_gate, g_final):
    b, s, d = x.shape
    t = b * s
    depth = w_in.shape[0]
    tables = _retention_tables(s)
    h = x.reshape(t, d)
    for i in range(depth):
        proj = _inproj(h, g_mix[i][None], w_in[i].astype(BF16))
        ret = _retention(proj, *tables, batch=b, seq=s)
        h1, a2t, qt = _mix(h, ret, proj, conv_w[i], conv_b[i][None], w_ret_up[i].astype(BF16),
                           w_conv_up[i].astype(BF16), w_out[i].astype(BF16), g_ffn[i][None],
                           w_peer_q[i].T.astype(BF16), seq=s)
        r2, e2, c1, e1 = _select(qt, peer_k1[i].astype(BF16), peer_k2[i].astype(BF16))
        h2 = _dense(a2t, peer_u[i].astype(BF16), peer_v[i].T.astype(BF16), r2, e2, c1, e1, h1)
        h = _final(h2, p[i].reshape(t, -1), g_ple[i][None], w_ple_proj[i].astype(BF16),
                   w_ple_gate[i].astype(BF16), g_final[None], last=i == depth - 1)
    return h.reshape(b, s, d)
```

```python
import functools

import numpy as np
import jax
import jax.numpy as jnp
from jax import lax
from jax.experimental import pallas as pl
from jax.experimental.pallas import tpu as pltpu

F32 = jnp.float32
BF16 = jnp.bfloat16
EPS = 1e-6

RET_HEADS = 4
RET_DK = 128
RET_DV = 256
RET_CHUNK = 128
ROPE_BASE = 10000.0
CONV_K = 3
PEER_HEADS = 8
PEER_NKEYS = 128
PEER_TOPK = 16

VMEM_LIMIT_BYTES = 56 * 1024 * 1024
LANES = 128


def _params(*sem):
    return pltpu.CompilerParams(dimension_semantics=sem, vmem_limit_bytes=VMEM_LIMIT_BYTES)


def _rms(x, g):
    return x * lax.rsqrt(jnp.mean(x * x, axis=-1, keepdims=True) + EPS) * g


def _dot(a, b):
    return jnp.dot(a, b, preferred_element_type=F32)


def _inproj_kernel(x_ref, g_ref, w_ref, o_ref):
    a = _rms(x_ref[...], g_ref[...]).astype(BF16)
    o_ref[...] = _dot(a, w_ref[...]).astype(o_ref.dtype)


def _inproj(x, g, w, *, tt=512):
    t, d = x.shape
    n = w.shape[1]
    return pl.pallas_call(
        _inproj_kernel,
        grid=(t // tt,),
        in_specs=[
            pl.BlockSpec((tt, d), lambda i: (i, 0)),
            pl.BlockSpec((1, d), lambda i: (0, 0)),
            pl.BlockSpec((d, n), lambda i: (0, 0), pipeline_mode=pl.Buffered(1)),
        ],
        out_specs=pl.BlockSpec((tt, n), lambda i: (i, 0)),
        out_shape=jax.ShapeDtypeStruct((t, n), BF16),
        compiler_params=_params("parallel"),
        name="inproj",
    )(x, g, w)


def _ret_kernel(q_ref, k_ref, v_ref, g_ref, cos_ref, sin_ref, dec_ref, xi_ref, zeta_ref, o_ref, st_ref, *, chunk):
    @pl.when(pl.program_id(1) == 0)
    def _():
        st_ref[...] = jnp.zeros_like(st_ref)

    scale = RET_DK ** -0.5
    half = RET_DK // 2

    def body(c, carry):
        r = pl.ds(pl.multiple_of(c * chunk, chunk), chunk)
        cosv = cos_ref[r, :]
        sinv = sin_ref[r, :]
        for h in range(RET_HEADS):
            qk = slice(h * RET_DK, (h + 1) * RET_DK)
            vv = slice(h * RET_DV, (h + 1) * RET_DV)
            q = q_ref[r, qk].astype(F32)
            k = k_ref[r, qk].astype(F32)
            q = q * cosv + pltpu.roll(q, half, 1) * sinv
            k = (k * cosv + pltpu.roll(k, half, 1) * sinv) * scale
            qb = q.astype(BF16)
            kb = k.astype(BF16)
            v = v_ref[r, vv]
            s = lax.dot_general(qb, kb, (((1,), (1,)), ((), ())), preferred_element_type=F32) * dec_ref[h]
            intra = _dot(s.astype(BF16), v)
            st = st_ref[h]
            cross = _dot(qb, st.astype(BF16)) * xi_ref[h]
            vz = (v.astype(F32) * zeta_ref[h]).astype(BF16)
            kv = lax.dot_general(kb, vz, (((0,), (0,)), ((), ())), preferred_element_type=F32)
            st_ref[h] = st * xi_ref[h, chunk - 1:chunk, :] + kv
            y = intra + cross
            yc = y - jnp.mean(y, axis=-1, keepdims=True)
            yn = yc * lax.rsqrt(jnp.mean(yc * yc, axis=-1, keepdims=True) + EPS)
            g = g_ref[r, vv].astype(F32)
            o_ref[r, vv] = (g * jax.nn.sigmoid(g) * yn).astype(o_ref.dtype)
        return carry

    lax.fori_loop(0, q_ref.shape[0] // chunk, body, 0)


def _retention(proj, cosf, sinf, dec, xi, zeta, *, batch, seq, sb=1024):
    t = proj.shape[0]
    h, c = RET_HEADS, RET_CHUNK
    sb = min(sb, seq)
    nsb = seq // sb
    qk_w, v_w = h * RET_DK, h * RET_DV
    assert 2 * qk_w == v_w

    def rows(col):
        return lambda b, s: (b * nsb + s, col)

    def const(shape):
        return pl.BlockSpec(shape, lambda b, s: (0,) * len(shape))

    return pl.pallas_call(
        functools.partial(_ret_kernel, chunk=c),
        grid=(batch, nsb),
        in_specs=[
            pl.BlockSpec((sb, qk_w), rows(0)),
            pl.BlockSpec((sb, qk_w), rows(1)),
            pl.BlockSpec((sb, v_w), rows(1)),
            pl.BlockSpec((sb, v_w), rows(2)),
            pl.BlockSpec((sb, RET_DK), lambda b, s: (s, 0)),
            pl.BlockSpec((sb, RET_DK), lambda b, s: (s, 0)),
            const((h, c, c)), const((h, c, RET_DV)), const((h, c, RET_DV)),
        ],
        out_specs=pl.BlockSpec((sb, v_w), lambda b, s: (b * nsb + s, 0)),
        out_shape=jax.ShapeDtypeStruct((t, v_w), BF16),
        scratch_shapes=[pltpu.VMEM((h, RET_DK, RET_DV), F32)],
        compiler_params=_params("parallel", "arbitrary"),
        name="retention",
    )(proj, proj, proj, proj, cosf, sinf, dec, xi, zeta)


def _retention_tables(seq):
    h, c = RET_HEADS, RET_CHUNK
    half = RET_DK // 2
    inv = ROPE_BASE ** (-jnp.arange(half, dtype=F32) / half)
    ang = jnp.arange(seq, dtype=F32)[:, None] * inv[None, :]
    cos, sin = jnp.cos(ang), jnp.sin(ang)
    cosf = jnp.concatenate([cos, cos], axis=-1)
    sinf = jnp.concatenate([-sin, sin], axis=-1)
    log_gamma = jnp.log1p(-(2.0 ** (-5.0 - jnp.arange(h, dtype=F32))))
    pos = jnp.arange(c, dtype=F32)
    diff = pos[:, None] - pos[None, :]
    causal = diff >= 0
    dec = jnp.where(causal[None], jnp.exp(log_gamma[:, None, None] * jnp.where(causal, diff, 0.0)[None]), 0.0)
    xi = jnp.exp(log_gamma[:, None] * (pos[None, :] + 1.0))
    zeta = jnp.exp(log_gamma[:, None] * (c - 1.0 - pos[None, :]))
    xi = jnp.broadcast_to(xi[:, :, None], (h, c, RET_DV))
    zeta = jnp.broadcast_to(zeta[:, :, None], (h, c, RET_DV))
    return cosf, sinf, dec, xi, zeta


def _mix_kernel(x_ref, ret_ref, cb_ref, cc_ref, ch_ref, gr_ref, gc_ref, cch_ref, chh_ref, cw_ref, cbias_ref,
                wru_ref, wcu_ref, wo_ref, gffn_ref, wqt_ref, h1_ref, a2t_ref, qt_ref, *, tiles_per_seq):
    i = pl.program_id(0)
    u = cc_ref[...].astype(F32) * ch_ref[...].astype(F32)
    halo = cch_ref[...].astype(F32) * chh_ref[...].astype(F32)
    halo = jnp.where(i % tiles_per_seq == 0, 0.0, halo)
    n_halo = halo.shape[0]
    p1 = halo[n_halo - 1:n_halo, :]
    p2 = halo[n_halo - 2:n_halo - 1, :]
    row = lax.broadcasted_iota(jnp.int32, u.shape, 0)
    u1 = jnp.where(row == 0, p1, pltpu.roll(u, 1, 0))
    u2 = jnp.where(row == 0, p2, jnp.where(row == 1, p1, pltpu.roll(u, 2, 0)))
    cw = cw_ref[...]
    z = cw[0:1, :] * u2 + cw[1:2, :] * u1 + cw[2:3, :] * u + cbias_ref[...]
    y_conv = _dot((cb_ref[...].astype(F32) * z).astype(BF16), wcu_ref[...])
    y_ret = _dot(ret_ref[...], wru_ref[...])
    merged = (jax.nn.sigmoid(gr_ref[...].astype(F32)) * y_ret
              + jax.nn.sigmoid(gc_ref[...].astype(F32)) * y_conv)
    h1 = x_ref[...] + _dot(merged.astype(BF16), wo_ref[...])
    h1_ref[...] = h1
    a2t = _rms(h1, gffn_ref[...]).T.astype(BF16)
    a2t_ref[...] = a2t
    qt_ref[...] = _dot(wqt_ref[...], a2t).astype(qt_ref.dtype)


def _mix(x, ret, proj, conv_w, conv_b, w_ret_up, w_conv_up, w_out, g_ffn, wqt, *, seq, tt=512):
    t, d = x.shape
    w = ret.shape[1]
    nq = wqt.shape[0]
    halo = 16
    blk = proj.shape[1] // w
    cb_i, cc_i, ch_i, gr_i, gc_i = blk - 5, blk - 4, blk - 3, blk - 2, blk - 1

    def tile(col):
        return pl.BlockSpec((tt, w), lambda i: (i, col))

    def halo_spec(col):
        return pl.BlockSpec((halo, w), lambda i: (jnp.maximum(i * (tt // halo) - 1, 0), col))

    def const(shape):
        return pl.BlockSpec(shape, lambda i: (0,) * len(shape), pipeline_mode=pl.Buffered(1))

    return pl.pallas_call(
        functools.partial(_mix_kernel, tiles_per_seq=seq // tt),
        grid=(t // tt,),
        in_specs=[
            pl.BlockSpec((tt, d), lambda i: (i, 0)),
            pl.BlockSpec((tt, w), lambda i: (i, 0)),
            tile(cb_i), tile(cc_i), tile(ch_i), tile(gr_i), tile(gc_i),
            halo_spec(cc_i), halo_spec(ch_i),
            const((CONV_K, w)), const((1, w)),
            const((w, d)), const((w, d)), const((d, d)), const((1, d)), const((nq, d)),
        ],
        out_specs=[
            pl.BlockSpec((tt, d), lambda i: (i, 0)),
            pl.BlockSpec((d, tt), lambda i: (0, i)),
            pl.BlockSpec((nq, tt), lambda i: (0, i)),
        ],
        out_shape=[
            jax.ShapeDtypeStruct((t, d), F32),
            jax.ShapeDtypeStruct((d, t), BF16),
            jax.ShapeDtypeStruct((nq, t), BF16),
        ],
        compiler_params=_params("parallel"),
        name="mix",
    )(x, ret, proj, proj, proj, proj, proj, proj, proj, conv_w, conv_b, w_ret_up, w_conv_up, w_out, g_ffn, wqt)


def _candidate_tables():
    k = PEER_TOPK
    pos, valid = [], []
    for j in range(k):
        pos.append(j)
        valid.append(True)
    for i in range(1, 8):
        for j in range(8):
            pos.append(i * k + j)
            valid.append((i + 1) * (j + 1) <= k)
    for i in range(8, k):
        pos.append(i * k)
        valid.append(True)
    pos = np.asarray(pos, np.float32)
    neg = np.where(np.asarray(valid), 0.0, -np.inf).astype(np.float32)
    return (np.ascontiguousarray(np.broadcast_to(pos[:, None], (pos.size, LANES))),
            np.ascontiguousarray(np.broadcast_to(neg[:, None], (neg.size, LANES))))


def _top_ranks(s):
    k = PEER_TOPK
    nkeys = s.shape[0]
    key = lax.broadcasted_iota(jnp.int32, s.shape, 0)
    row = lax.broadcasted_iota(jnp.int32, (k, s.shape[1]), 0)
    rank = jnp.full(s.shape, float(k), F32)
    vals = jnp.zeros((k, s.shape[1]), F32)
    work = s
    for r in range(k):
        m = jnp.max(work, axis=0, keepdims=True)
        first = jnp.min(jnp.where(work == m, key, nkeys), axis=0, keepdims=True)
        sel = key == first
        rank = jnp.where(sel, float(r), rank)
        work = jnp.where(sel, -jnp.inf, work)
        vals = jnp.where(row == r, m, vals)
    return rank, vals


def _sort16_network():
    def merge(lo, hi, r):
        step = r * 2
        if step < hi - lo:
            yield from merge(lo, hi, step)
            yield from merge(lo + r, hi, step)
            yield from [(i, i + r) for i in range(lo + r, hi - r, step)]
        else:
            yield (lo, lo + r)

    def sort(lo, hi):
        if hi - lo >= 1:
            mid = lo + (hi - lo) // 2
            yield from sort(lo, mid)
            yield from sort(mid + 1, hi)
            yield from merge(lo, hi, 1)

    return tuple(sort(0, PEER_TOPK - 1))


def _sublane_allreduce(x, op):
    for shift in (4, 2, 1):
        x = op(x, pltpu.roll(x, shift, 0))
    return x


def _sorted_top_values(s):
    k = PEER_TOPK
    x = [s[8 * i:8 * i + 8, :] for i in range(s.shape[0] // 8)]
    assert len(x) == k
    for i, j in _sort16_network():
        x[i], x[j] = jnp.maximum(x[i], x[j]), jnp.minimum(x[i], x[j])
    for shift in (4, 2, 1):
        p = [pltpu.roll(v, shift, 0) for v in x]
        y = [jnp.maximum(x[r], p[k - 1 - r]) for r in range(k)]
        d = k // 2
        while d:
            for i in range(k):
                if not i & d:
                    y[i], y[i + d] = jnp.maximum(y[i], y[i + d]), jnp.minimum(y[i], y[i + d])
            d //= 2
        x = y
    return x


_SORT10_NETWORK = ((0, 5), (1, 6), (2, 7), (3, 8), (4, 9), (0, 3), (1, 4), (5, 8), (6, 9), (0, 2), (3, 6), (7, 9),
                   (0, 1), (2, 4), (5, 7), (8, 9), (1, 2), (3, 5), (4, 6), (7, 8), (1, 3), (2, 5), (4, 7), (6, 8),
                   (2, 3), (6, 7), (3, 4), (5, 6), (4, 5))


def _bitonic_sort(y):
    d = len(y) // 2
    while d:
        for i in range(len(y)):
            if not i & d:
                y[i], y[i + d] = jnp.maximum(y[i], y[i + d]), jnp.minimum(y[i], y[i + d])
        d //= 2
    return y


def _kth_largest(cand):
    k = PEER_TOPK
    x = list(cand)
    assert len(x) == 10
    for i, j in _SORT10_NETWORK:
        x[i], x[j] = jnp.maximum(x[i], x[j]), jnp.minimum(x[i], x[j])
    x = x + [None] * (k - len(x))
    for shift in (4, 2, 1):
        y = []
        for r in range(k):
            own, other = x[r], x[k - 1 - r]
            other = None if other is None else pltpu.roll(other, shift, 0)
            if own is None or other is None:
                y.append(own if other is None else other)
            else:
                y.append(jnp.maximum(own, other))
        assert all(v is not None for v in y)
        if shift == 1:
            tau = y[0]
            for v in y[1:]:
                tau = jnp.minimum(tau, v)
            return tau
        x = _bitonic_sort(y)


def _count_leading(b, pred):
    m8 = pred(b[7])
    m4 = pred(jnp.where(m8, b[11], b[3]))
    m2 = pred(jnp.where(m8, jnp.where(m4, b[13], b[9]), jnp.where(m4, b[5], b[1])))
    m1 = pred(jnp.where(m8, jnp.where(m4, jnp.where(m2, b[14], b[12]), jnp.where(m2, b[10], b[8])),
                        jnp.where(m4, jnp.where(m2, b[6], b[4]), jnp.where(m2, b[2], b[0]))))
    cnt = (jnp.where(m8, 8.0, 0.0) + jnp.where(m4, 4.0, 0.0)) + (jnp.where(m2, 2.0, 0.0) + jnp.where(m1, 1.0, 0.0))
    return jnp.where(pred(b[15]), 16.0, cnt)


def _select_fast(s1, s2):
    k = PEER_TOPK
    assert k == 16
    n = s1.shape[1]
    a = _sorted_top_values(s1)
    b = _sorted_top_values(s2)
    row = lax.broadcasted_iota(jnp.int32, (8, n), 0)

    def rows_of(vals):
        out = vals[0]
        for r in range(1, 8):
            out = jnp.where(row == r, vals[r], out)
        return out

    b_lo, b_hi, a_hi = rows_of(b[0:8]), rows_of(b[8:16]), rows_of(a[8:16])
    neg_inf = jnp.float32(-jnp.inf)
    cand = [a[0] + b_lo, a[0] + b_hi]
    for i in range(1, 8):
        cand.append(jnp.where(row < k // (i + 1), a[i] + b_lo, neg_inf))
    cand.append(a_hi + b[0])
    tau = _kth_largest(cand)
    top = a[0] + b[0]
    z = None
    for c in cand:
        term = jnp.where(c >= tau, jnp.exp(c - top), 0.0)
        z = term if z is None else z + term
    inv_z = 1.0 / _sublane_allreduce(z, jnp.add)

    rank2, count1, e1, e2 = [], [], [], []
    n_top2 = None
    n_pairs = None
    for i in range(s1.shape[0] // 8):
        x1 = s1[8 * i:8 * i + 8, :]
        x2 = s2[8 * i:8 * i + 8, :]
        r2 = _count_leading(b, lambda t: t > x2)
        c1 = _count_leading(b, lambda t: x1 + t >= tau)
        rank2.append(r2)
        count1.append(c1)
        e1.append(jnp.exp(x1 - a[0]) * inv_z)
        e2.append(jnp.exp(x2 - b[0]))
        in_top = jnp.where(r2 < float(k), 1.0, 0.0)
        n_top2 = in_top if n_top2 is None else n_top2 + in_top
        n_pairs = c1 if n_pairs is None else n_pairs + c1
    n_top2 = _sublane_allreduce(n_top2, jnp.add)
    n_pairs = _sublane_allreduce(n_pairs, jnp.add)
    suspicious = jnp.where((n_top2 != float(k)) | (n_pairs != float(k)), 1.0, 0.0)
    cat = lambda xs: jnp.concatenate(xs, axis=0)
    return cat(rank2), cat(count1), cat(e1), cat(e2), suspicious


def _select_exact(s1, s2, pos, neg):
    k = PEER_TOPK
    rank1, a = _top_ranks(s1)
    rank2, b = _top_ranks(s2)
    pieces = [a[0:1, :] + b]
    for i in range(1, 8):
        pieces.append(a[i:i + 1, :] + b[0:8, :])
    pieces.append(a[8:k, :] + b[0:1, :])
    cand = jnp.concatenate(pieces, axis=0) + neg
    work = cand
    chosen = jnp.zeros(cand.shape, F32)
    for r in range(k):
        m = jnp.max(work, axis=0, keepdims=True)
        first = jnp.min(jnp.where(work == m, pos, float(k * k)), axis=0, keepdims=True)
        sel = pos == first
        chosen = jnp.where(sel, 1.0, chosen)
        work = jnp.where(sel, -jnp.inf, work)
    top = a[0:1, :] + b[0:1, :]
    z = jnp.sum(jnp.where(chosen > 0.0, jnp.exp(cand - top), 0.0), axis=0, keepdims=True)
    row8 = lax.broadcasted_iota(jnp.int32, (8, LANES), 0)
    cnt_lo = jnp.where(row8 == 0, jnp.sum(chosen[0:k, :], axis=0, keepdims=True), 0.0)
    for i in range(1, 8):
        lo = k + 8 * (i - 1)
        cnt_lo = jnp.where(row8 == i, jnp.sum(chosen[lo:lo + 8, :], axis=0, keepdims=True), cnt_lo)
    cnt = jnp.concatenate([cnt_lo, chosen[k + 56:k + 64, :]], axis=0)
    count1 = jnp.zeros(s1.shape, F32)
    for r in range(k):
        count1 = jnp.where(rank1 == float(r), cnt[r:r + 1, :], count1)
    return rank2, count1, jnp.exp(s1 - a[0:1, :]) / z, jnp.exp(s2 - b[0:1, :])


def _select_kernel(qt_ref, k1_ref, k2_ref, pos_ref, neg_ref, r2_ref, e2_ref, c1_ref, e1_ref, s1_ref, s2_ref):
    half = PEER_NKEYS
    s1_ref[...] = _dot(k1_ref[0], qt_ref[0:half, :])
    s2_ref[...] = _dot(k2_ref[0], qt_ref[half:2 * half, :])

    groups_per_iter = 2

    def body(it, carry):
        groups = [it * groups_per_iter + i for i in range(groups_per_iter)]
        lanes = [pl.ds(pl.multiple_of(g * LANES, LANES), LANES) for g in groups]

        def emit(g, ln, rank2, count1, e1, e2):
            r2_ref[0, :, ln] = rank2.astype(r2_ref.dtype)
            c1_ref[0, g] = count1
            e1_ref[0, g] = e1
            e2_ref[0, :, ln] = e2.astype(e2_ref.dtype)

        fast = [_select_fast(s1_ref[:, ln], s2_ref[:, ln]) for ln in lanes]
        for g, ln, (*tables, _) in zip(groups, lanes, fast):
            emit(g, ln, *tables)
        for g, ln, (*_, suspicious) in zip(groups, lanes, fast):
            @pl.when(jnp.max(suspicious) > 0.0)
            def _():
                emit(g, ln, *_select_exact(s1_ref[:, ln], s2_ref[:, ln], pos_ref[...], neg_ref[...]))

        return carry

    lax.fori_loop(0, s1_ref.shape[1] // (LANES * groups_per_iter), body, 0)


def _select(qt, k1, k2, *, ts=1024):
    nq, t = qt.shape
    h = PEER_HEADS
    nk = PEER_NKEYS
    pos, neg = _candidate_tables()
    out_spec = pl.BlockSpec((1, nk, ts), lambda i, j: (j, 0, i))
    row_spec = pl.BlockSpec((1, ts // LANES, nk, LANES), lambda i, j: (j, i, 0, 0))
    return pl.pallas_call(
        _select_kernel,
        grid=(t // ts, h),
        in_specs=[
            pl.BlockSpec((nq // h, ts), lambda i, j: (j, i)),
            pl.BlockSpec((1, nk, nk), lambda i, j: (j, 0, 0)),
            pl.BlockSpec((1, nk, nk), lambda i, j: (j, 0, 0)),
            pl.BlockSpec(pos.shape, lambda i, j: (0, 0)),
            pl.BlockSpec(neg.shape, lambda i, j: (0, 0)),
        ],
        out_specs=[out_spec, out_spec, row_spec, row_spec],
        out_shape=[
            jax.ShapeDtypeStruct((h, nk, t), BF16),
            jax.ShapeDtypeStruct((h, nk, t), BF16),
            jax.ShapeDtypeStruct((h, t // LANES, nk, LANES), F32),
            jax.ShapeDtypeStruct((h, t // LANES, nk, LANES), F32),
        ],
        scratch_shapes=[pltpu.VMEM((nk, ts), F32), pltpu.VMEM((nk, ts), F32)],
        compiler_params=_params("parallel", "parallel"),
        name="peer_select",
    )(qt, k1, k2, jnp.asarray(pos), jnp.asarray(neg))


def _dense_kernel(a2t_ref, u_ref, unext_ref, vt_ref, r2_ref, e2_ref, c1_ref, e1_ref, h1_ref,
                  p_ref, gple_ref, wproj_ref, wgate_ref, gfin_ref, o_ref, acc_ref,
                  act_even_ref, act_odd_ref, *, n_pieces, last):
    j = pl.program_id(1)
    nk = PEER_NKEYS
    te = u_ref.shape[0]
    tt = a2t_ref.shape[1]
    rows = te // n_pieces
    pack = 16

    def activations(lhs):
        return _dot(lhs.astype(BF16), a2t_ref[...]).astype(BF16)

    @pl.when(j == 0)
    def _():
        acc_ref[...] = jnp.zeros_like(acc_ref)
        act_even_ref[...] = activations(u_ref[0:rows, :])

    def gated_piece(p, act):
        pieces = []
        for kk in range(rows // nk):
            i1 = (j * n_pieces + p) * (rows // nk) + kk

            def key_row(ref, h):
                parts = [jnp.broadcast_to(ref[h, g, pl.ds(i1, 1), :], (pack, LANES)) for g in range(tt // LANES)]
                row = jnp.concatenate(parts, axis=1).astype(BF16)
                return jnp.concatenate([row] * (nk // pack), axis=0)

            w = None
            for h in range(PEER_HEADS):
                term = jnp.where(r2_ref[h] < key_row(c1_ref, h), e2_ref[h], jnp.zeros((), BF16)) * key_row(e1_ref, h)
                w = term if w is None else w + term
            a = act[kk * nk:(kk + 1) * nk, :]
            half = jnp.asarray(0.5, a.dtype)
            gelu = (half * a) * (jnp.asarray(1.0, a.dtype) + lax.erf(a * jnp.asarray(2.0 ** -0.5, a.dtype)))
            pieces.append(w * gelu)
        return jnp.concatenate(pieces, axis=0)

    def step(act_ref, next_act_ref):
        act = act_ref[...]
        total = None
        for p in range(n_pieces):
            if p + 1 < n_pieces:
                next_act = activations(u_ref[(p + 1) * rows:(p + 2) * rows, :])
            part = _dot(vt_ref[:, p * rows:(p + 1) * rows], gated_piece(p, act))
            total = part if total is None else total + part
            act = next_act
        acc_ref[...] += total
        next_act_ref[...] = activations(unext_ref[...])

    @pl.when(j % 2 == 0)
    def _():
        step(act_even_ref, act_odd_ref)

    @pl.when(j % 2 == 1)
    def _():
        step(act_odd_ref, act_even_ref)

    @pl.when(j == pl.num_programs(1) - 1)
    def _():
        h2 = h1_ref[...] + acc_ref[...].T
        ple = _dot(p_ref[...].astype(BF16), wproj_ref[...])
        gate = jax.nn.sigmoid(_dot(_rms(h2, gple_ref[...]).astype(BF16), wgate_ref[...]))
        h3 = h2 + gate * ple
        o_ref[...] = _rms(h3, gfin_ref[...]) if last else h3


def _dense(a2t, u, vt, r2, e2, c1, e1, h1, p, g_ple, w_proj, w_gate, g_final, *, last, tt=512, te=1024, n_pieces=4):
    d, t = a2t.shape
    ne = u.shape[0]
    pd = p.shape[1]
    h, nk = PEER_HEADS, PEER_NKEYS
    nb = ne // te
    tab = pl.BlockSpec((h, nk, tt), lambda i, j: (0, 0, i))
    rowtab = pl.BlockSpec((h, tt // LANES, nk, LANES), lambda i, j: (0, i, 0, 0))

    def const(shape):
        return pl.BlockSpec(shape, lambda i, j: (0,) * len(shape), pipeline_mode=pl.Buffered(1))

    return pl.pallas_call(
        functools.partial(_dense_kernel, n_pieces=n_pieces, last=last),
        grid=(t // tt, nb),
        in_specs=[
            pl.BlockSpec((d, tt), lambda i, j: (0, i)),
            pl.BlockSpec((te, d), lambda i, j: (j, 0)),
            pl.BlockSpec((te // n_pieces, d), lambda i, j: (jnp.minimum(j + 1, nb - 1) * n_pieces, 0)),
            pl.BlockSpec((d, te), lambda i, j: (0, j)),
            tab, tab, rowtab, rowtab,
            pl.BlockSpec((tt, d), lambda i, j: (i, 0)),
            pl.BlockSpec((tt, pd), lambda i, j: (i, 0)),
            const((1, d)), const((pd, d)), const((d, d)), const((1, d)),
        ],
        out_specs=pl.BlockSpec((tt, d), lambda i, j: (i, 0)),
        out_shape=jax.ShapeDtypeStruct((t, d), F32),
        scratch_shapes=[pltpu.VMEM((d, tt), F32), pltpu.VMEM((te // n_pieces, tt), BF16),
                        pltpu.VMEM((te // n_pieces, tt), BF16)],
        compiler_params=_params("parallel", "arbitrary"),
        name="peer_dense",
    )(a2t, u, u, vt, r2, e2, c1, e1, h1, p, g_ple, w_proj, w_gate, g_final)


def kernel(x, p, g_mix, w_in, conv_w, conv_b, w_ret_up, w_conv_up, w_out, g_ffn, w_peer_q, peer_k1, peer_k2, peer_u, peer_v, g_ple, w_ple_proj, w_ple_gate, g_final):
    b, s, d = x.shape
    t = b * s
    depth = w_in.shape[0]
    tables = _retention_tables(s)
    h = x.reshape(t, d)
    for i in range(depth):
        proj = _inproj(h, g_mix[i][None], w_in[i].astype(BF16))
        ret = _retention(proj, *tables, batch=b, seq=s)
        h1, a2t, qt = _mix(h, ret, proj, conv_w[i], conv_b[i][None], w_ret_up[i].astype(BF16),
                           w_conv_up[i].astype(BF16), w_out[i].astype(BF16), g_ffn[i][None],
                           w_peer_q[i].T.astype(BF16), seq=s)
        r2, e2, c1, e1 = _select(qt, peer_k1[i].astype(BF16), peer_k2[i].astype(BF16))
        h = _dense(a2t, peer_u[i], peer_v[i].T.astype(BF16), r2, e2, c1, e1, h1,
                   p[i].reshape(t, -1), g_ple[i][None], w_ple_proj[i].astype(BF16),
                   w_ple_gate[i].astype(BF16), g_final[None], last=i == depth - 1)
    return h.reshape(b, s, d)
```

```python
import functools

import numpy as np
import jax
import jax.numpy as jnp
from jax import lax
from jax.experimental import pallas as pl
from jax.experimental.pallas import tpu as pltpu

F32 = jnp.float32
BF16 = jnp.bfloat16
EPS = 1e-6

RET_HEADS = 4
RET_DK = 128
RET_DV = 256
RET_CHUNK = 128
ROPE_BASE = 10000.0
CONV_K = 3
PEER_HEADS = 8
PEER_NKEYS = 128
PEER_TOPK = 16

VMEM_LIMIT_BYTES = 56 * 1024 * 1024
LANES = 128


def _params(*sem):
    return pltpu.CompilerParams(dimension_semantics=sem, vmem_limit_bytes=VMEM_LIMIT_BYTES)


def _rms(x, g):
    return x * lax.rsqrt(jnp.mean(x * x, axis=-1, keepdims=True) + EPS) * g


def _dot(a, b):
    return jnp.dot(a, b, preferred_element_type=F32)


def _inproj_kernel(x_ref, g_ref, w_ref, o_ref):
    a = _rms(x_ref[...], g_ref[...]).astype(BF16)
    o_ref[...] = _dot(a, w_ref[...]).astype(o_ref.dtype)


def _inproj(x, g, w, *, tt=512):
    t, d = x.shape
    n = w.shape[1]
    return pl.pallas_call(
        _inproj_kernel,
        grid=(t // tt,),
        in_specs=[
            pl.BlockSpec((tt, d), lambda i: (i, 0)),
            pl.BlockSpec((1, d), lambda i: (0, 0)),
            pl.BlockSpec((d, n), lambda i: (0, 0), pipeline_mode=pl.Buffered(1)),
        ],
        out_specs=pl.BlockSpec((tt, n), lambda i: (i, 0)),
        out_shape=jax.ShapeDtypeStruct((t, n), BF16),
        compiler_params=_params("parallel"),
        name="inproj",
    )(x, g, w)


def _ret_kernel(q_ref, k_ref, v_ref, g_ref, cos_ref, sin_ref, dec_ref, xi_ref, zeta_ref, o_ref, st_ref, *, chunk):
    @pl.when(pl.program_id(1) == 0)
    def _():
        st_ref[...] = jnp.zeros_like(st_ref)

    scale = RET_DK ** -0.5
    half = RET_DK // 2

    def body(c, carry):
        r = pl.ds(pl.multiple_of(c * chunk, chunk), chunk)
        cosv = cos_ref[r, :]
        sinv = sin_ref[r, :]
        for bb in range(q_ref.shape[0]):
            for h in range(RET_HEADS):
                qk = slice(h * RET_DK, (h + 1) * RET_DK)
                vv = slice(h * RET_DV, (h + 1) * RET_DV)
                q = q_ref[bb, r, qk].astype(F32)
                k = k_ref[bb, r, qk].astype(F32)
                q = q * cosv + pltpu.roll(q, half, 1) * sinv
                k = (k * cosv + pltpu.roll(k, half, 1) * sinv) * scale
                qb = q.astype(BF16)
                kb = k.astype(BF16)
                v = v_ref[bb, r, vv]
                s = lax.dot_general(qb, kb, (((1,), (1,)), ((), ())), preferred_element_type=F32) * dec_ref[h]
                intra = _dot(s.astype(BF16), v)
                st = st_ref[bb, h]
                cross = _dot(qb, st.astype(BF16)) * xi_ref[h]
                vz = (v.astype(F32) * zeta_ref[h]).astype(BF16)
                kv = lax.dot_general(kb, vz, (((0,), (0,)), ((), ())), preferred_element_type=F32)
                st_ref[bb, h] = st * xi_ref[h, chunk - 1:chunk, :] + kv
                y = intra + cross
                yc = y - jnp.mean(y, axis=-1, keepdims=True)
                yn = yc * lax.rsqrt(jnp.mean(yc * yc, axis=-1, keepdims=True) + EPS)
                g = g_ref[bb, r, vv].astype(F32)
                o_ref[bb, r, vv] = (g * jax.nn.sigmoid(g) * yn).astype(o_ref.dtype)
        return carry

    lax.fori_loop(0, q_ref.shape[1] // chunk, body, 0)


def _retention(proj, cosf, sinf, dec, xi, zeta, *, batch, seq, sb=1024, seqs_per_step=2):
    t, width = proj.shape
    h, c = RET_HEADS, RET_CHUNK
    sb = min(sb, seq)
    bs = seqs_per_step if batch % seqs_per_step == 0 else 1
    qk_w, v_w = h * RET_DK, h * RET_DV
    assert 2 * qk_w == v_w
    proj3 = proj.reshape(batch, seq, width)

    def rows(col):
        return lambda b, s: (b, s, col)

    def const(shape):
        return pl.BlockSpec(shape, lambda b, s: (0,) * len(shape))

    out = pl.pallas_call(
        functools.partial(_ret_kernel, chunk=c),
        grid=(batch // bs, seq // sb),
        in_specs=[
            pl.BlockSpec((bs, sb, qk_w), rows(0)),
            pl.BlockSpec((bs, sb, qk_w), rows(1)),
            pl.BlockSpec((bs, sb, v_w), rows(1)),
            pl.BlockSpec((bs, sb, v_w), rows(2)),
            pl.BlockSpec((sb, RET_DK), lambda b, s: (s, 0)),
            pl.BlockSpec((sb, RET_DK), lambda b, s: (s, 0)),
            const((h, c, c)), const((h, c, RET_DV)), const((h, c, RET_DV)),
        ],
        out_specs=pl.BlockSpec((bs, sb, v_w), lambda b, s: (b, s, 0)),
        out_shape=jax.ShapeDtypeStruct((batch, seq, v_w), BF16),
        scratch_shapes=[pltpu.VMEM((bs, h, RET_DK, RET_DV), F32)],
        compiler_params=_params("parallel", "arbitrary"),
        name="retention",
    )(proj3, proj3, proj3, proj3, cosf, sinf, dec, xi, zeta)
    return out.reshape(t, v_w)


def _retention_tables(seq):
    h, c = RET_HEADS, RET_CHUNK
    half = RET_DK // 2
    inv = ROPE_BASE ** (-jnp.arange(half, dtype=F32) / half)
    ang = jnp.arange(seq, dtype=F32)[:, None] * inv[None, :]
    cos, sin = jnp.cos(ang), jnp.sin(ang)
    cosf = jnp.concatenate([cos, cos], axis=-1)
    sinf = jnp.concatenate([-sin, sin], axis=-1)
    log_gamma = jnp.log1p(-(2.0 ** (-5.0 - jnp.arange(h, dtype=F32))))
    pos = jnp.arange(c, dtype=F32)
    diff = pos[:, None] - pos[None, :]
    causal = diff >= 0
    dec = jnp.where(causal[None], jnp.exp(log_gamma[:, None, None] * jnp.where(causal, diff, 0.0)[None]), 0.0)
    xi = jnp.exp(log_gamma[:, None] * (pos[None, :] + 1.0))
    zeta = jnp.exp(log_gamma[:, None] * (c - 1.0 - pos[None, :]))
    xi = jnp.broadcast_to(xi[:, :, None], (h, c, RET_DV))
    zeta = jnp.broadcast_to(zeta[:, :, None], (h, c, RET_DV))
    return cosf, sinf, dec, xi, zeta


def _mix_kernel(x_ref, ret_ref, cb_ref, cc_ref, ch_ref, gr_ref, gc_ref, cch_ref, chh_ref, cw_ref, cbias_ref,
                wru_ref, wcu_ref, wo_ref, gffn_ref, wqt_ref, h1_ref, a2t_ref, qt_ref, *, tiles_per_seq):
    i = pl.program_id(0)
    u = cc_ref[...].astype(F32) * ch_ref[...].astype(F32)
    halo = cch_ref[...].astype(F32) * chh_ref[...].astype(F32)
    halo = jnp.where(i % tiles_per_seq == 0, 0.0, halo)
    n_halo = halo.shape[0]
    p1 = halo[n_halo - 1:n_halo, :]
    p2 = halo[n_halo - 2:n_halo - 1, :]
    row = lax.broadcasted_iota(jnp.int32, u.shape, 0)
    u1 = jnp.where(row == 0, p1, pltpu.roll(u, 1, 0))
    u2 = jnp.where(row == 0, p2, jnp.where(row == 1, p1, pltpu.roll(u, 2, 0)))
    cw = cw_ref[...]
    z = cw[0:1, :] * u2 + cw[1:2, :] * u1 + cw[2:3, :] * u + cbias_ref[...]
    y_conv = _dot((cb_ref[...].astype(F32) * z).astype(BF16), wcu_ref[...])
    y_ret = _dot(ret_ref[...], wru_ref[...])
    merged = (jax.nn.sigmoid(gr_ref[...].astype(F32)) * y_ret
              + jax.nn.sigmoid(gc_ref[...].astype(F32)) * y_conv)
    h1 = x_ref[...] + _dot(merged.astype(BF16), wo_ref[...])
    h1_ref[...] = h1
    a2t = _rms(h1, gffn_ref[...]).T.astype(BF16)
    a2t_ref[...] = a2t
    qt_ref[...] = _dot(wqt_ref[...], a2t).astype(qt_ref.dtype)


def _mix(x, ret, proj, conv_w, conv_b, w_ret_up, w_conv_up, w_out, g_ffn, wqt, *, seq, tt=512):
    t, d = x.shape
    w = ret.shape[1]
    nq = wqt.shape[0]
    halo = 16
    blk = proj.shape[1] // w
    cb_i, cc_i, ch_i, gr_i, gc_i = blk - 5, blk - 4, blk - 3, blk - 2, blk - 1

    def tile(col):
        return pl.BlockSpec((tt, w), lambda i: (i, col))

    def halo_spec(col):
        return pl.BlockSpec((halo, w), lambda i: (jnp.maximum(i * (tt // halo) - 1, 0), col))

    def const(shape):
        return pl.BlockSpec(shape, lambda i: (0,) * len(shape), pipeline_mode=pl.Buffered(1))

    return pl.pallas_call(
        functools.partial(_mix_kernel, tiles_per_seq=seq // tt),
        grid=(t // tt,),
        in_specs=[
            pl.BlockSpec((tt, d), lambda i: (i, 0)),
            pl.BlockSpec((tt, w), lambda i: (i, 0)),
            tile(cb_i), tile(cc_i), tile(ch_i), tile(gr_i), tile(gc_i),
            halo_spec(cc_i), halo_spec(ch_i),
            const((CONV_K, w)), const((1, w)),
            const((w, d)), const((w, d)), const((d, d)), const((1, d)), const((nq, d)),
        ],
        out_specs=[
            pl.BlockSpec((tt, d), lambda i: (i, 0)),
            pl.BlockSpec((d, tt), lambda i: (0, i)),
            pl.BlockSpec((nq, tt), lambda i: (0, i)),
        ],
        out_shape=[
            jax.ShapeDtypeStruct((t, d), F32),
            jax.ShapeDtypeStruct((d, t), BF16),
            jax.ShapeDtypeStruct((nq, t), BF16),
        ],
        compiler_params=_params("parallel"),
        name="mix",
    )(x, ret, proj, proj, proj, proj, proj, proj, proj, conv_w, conv_b, w_ret_up, w_conv_up, w_out, g_ffn, wqt)


def _candidate_tables():
    k = PEER_TOPK
    pos, valid = [], []
    for j in range(k):
        pos.append(j)
        valid.append(True)
    for i in range(1, 8):
        for j in range(8):
            pos.append(i * k + j)
            valid.append((i + 1) * (j + 1) <= k)
    for i in range(8, k):
        pos.append(i * k)
        valid.append(True)
    pos = np.asarray(pos, np.float32)
    neg = np.where(np.asarray(valid), 0.0, -np.inf).astype(np.float32)
    return (np.ascontiguousarray(np.broadcast_to(pos[:, None], (pos.size, LANES))),
            np.ascontiguousarray(np.broadcast_to(neg[:, None], (neg.size, LANES))))


def _top_ranks(s):
    k = PEER_TOPK
    nkeys = s.shape[0]
    key = lax.broadcasted_iota(jnp.int32, s.shape, 0)
    row = lax.broadcasted_iota(jnp.int32, (k, s.shape[1]), 0)
    rank = jnp.full(s.shape, float(k), F32)
    vals = jnp.zeros((k, s.shape[1]), F32)
    work = s
    for r in range(k):
        m = jnp.max(work, axis=0, keepdims=True)
        first = jnp.min(jnp.where(work == m, key, nkeys), axis=0, keepdims=True)
        sel = key == first
        rank = jnp.where(sel, float(r), rank)
        work = jnp.where(sel, -jnp.inf, work)
        vals = jnp.where(row == r, m, vals)
    return rank, vals


def _sort16_network():
    def merge(lo, hi, r):
        step = r * 2
        if step < hi - lo:
            yield from merge(lo, hi, step)
            yield from merge(lo + r, hi, step)
            yield from [(i, i + r) for i in range(lo + r, hi - r, step)]
        else:
            yield (lo, lo + r)

    def sort(lo, hi):
        if hi - lo >= 1:
            mid = lo + (hi - lo) // 2
            yield from sort(lo, mid)
            yield from sort(mid + 1, hi)
            yield from merge(lo, hi, 1)

    return tuple(sort(0, PEER_TOPK - 1))


def _sublane_allreduce(x, op):
    for shift in (4, 2, 1):
        x = op(x, pltpu.roll(x, shift, 0))
    return x


def _sorted_top_values(s):
    k = PEER_TOPK
    x = [s[8 * i:8 * i + 8, :] for i in range(s.shape[0] // 8)]
    assert len(x) == k
    for i, j in _sort16_network():
        x[i], x[j] = jnp.maximum(x[i], x[j]), jnp.minimum(x[i], x[j])
    for shift in (4, 2, 1):
        p = [pltpu.roll(v, shift, 0) for v in x]
        y = [jnp.maximum(x[r], p[k - 1 - r]) for r in range(k)]
        d = k // 2
        while d:
            for i in range(k):
                if not i & d:
                    y[i], y[i + d] = jnp.maximum(y[i], y[i + d]), jnp.minimum(y[i], y[i + d])
            d //= 2
        x = y
    return x


_SORT10_NETWORK = ((0, 5), (1, 6), (2, 7), (3, 8), (4, 9), (0, 3), (1, 4), (5, 8), (6, 9), (0, 2), (3, 6), (7, 9),
                   (0, 1), (2, 4), (5, 7), (8, 9), (1, 2), (3, 5), (4, 6), (7, 8), (1, 3), (2, 5), (4, 7), (6, 8),
                   (2, 3), (6, 7), (3, 4), (5, 6), (4, 5))


def _bitonic_sort(y):
    d = len(y) // 2
    while d:
        for i in range(len(y)):
            if not i & d:
                y[i], y[i + d] = jnp.maximum(y[i], y[i + d]), jnp.minimum(y[i], y[i + d])
        d //= 2
    return y


def _kth_largest(cand):
    k = PEER_TOPK
    x = list(cand)
    assert len(x) == 10
    for i, j in _SORT10_NETWORK:
        x[i], x[j] = jnp.maximum(x[i], x[j]), jnp.minimum(x[i], x[j])
    x = x + [None] * (k - len(x))
    for shift in (4, 2, 1):
        y = []
        for r in range(k):
            own, other = x[r], x[k - 1 - r]
            other = None if other is None else pltpu.roll(other, shift, 0)
            if own is None or other is None:
                y.append(own if other is None else other)
            else:
                y.append(jnp.maximum(own, other))
        assert all(v is not None for v in y)
        if shift == 1:
            tau = y[0]
            for v in y[1:]:
                tau = jnp.minimum(tau, v)
            return tau
        x = _bitonic_sort(y)


def _count_leading(b, pred):
    m8 = pred(b[7])
    m4 = pred(jnp.where(m8, b[11], b[3]))
    m2 = pred(jnp.where(m8, jnp.where(m4, b[13], b[9]), jnp.where(m4, b[5], b[1])))
    m1 = pred(jnp.where(m8, jnp.where(m4, jnp.where(m2, b[14], b[12]), jnp.where(m2, b[10], b[8])),
                        jnp.where(m4, jnp.where(m2, b[6], b[4]), jnp.where(m2, b[2], b[0]))))
    cnt = (jnp.where(m8, 8.0, 0.0) + jnp.where(m4, 4.0, 0.0)) + (jnp.where(m2, 2.0, 0.0) + jnp.where(m1, 1.0, 0.0))
    return jnp.where(pred(b[15]), 16.0, cnt)


def _select_fast(s1, s2):
    k = PEER_TOPK
    assert k == 16
    n = s1.shape[1]
    a = _sorted_top_values(s1)
    b = _sorted_top_values(s2)
    row = lax.broadcasted_iota(jnp.int32, (8, n), 0)

    def rows_of(vals):
        out = vals[0]
        for r in range(1, 8):
            out = jnp.where(row == r, vals[r], out)
        return out

    b_lo, b_hi, a_hi = rows_of(b[0:8]), rows_of(b[8:16]), rows_of(a[8:16])
    neg_inf = jnp.float32(-jnp.inf)
    cand = [a[0] + b_lo, a[0] + b_hi]
    for i in range(1, 8):
        cand.append(jnp.where(row < k // (i + 1), a[i] + b_lo, neg_inf))
    cand.append(a_hi + b[0])
    tau = _kth_largest(cand)
    top = a[0] + b[0]
    z = None
    for c in cand:
        term = jnp.where(c >= tau, jnp.exp(c - top), 0.0)
        z = term if z is None else z + term
    inv_z = 1.0 / _sublane_allreduce(z, jnp.add)

    rank2, count1, e1, e2 = [], [], [], []
    n_top2 = None
    n_pairs = None
    for i in range(s1.shape[0] // 8):
        x1 = s1[8 * i:8 * i + 8, :]
        x2 = s2[8 * i:8 * i + 8, :]
        r2 = _count_leading(b, lambda t: t > x2)
        c1 = _count_leading(b, lambda t: x1 + t >= tau)
        rank2.append(r2)
        count1.append(c1)
        e1.append(jnp.exp(x1 - a[0]) * inv_z)
        e2.append(jnp.exp(x2 - b[0]))
        in_top = jnp.where(r2 < float(k), 1.0, 0.0)
        n_top2 = in_top if n_top2 is None else n_top2 + in_top
        n_pairs = c1 if n_pairs is None else n_pairs + c1
    n_top2 = _sublane_allreduce(n_top2, jnp.add)
    n_pairs = _sublane_allreduce(n_pairs, jnp.add)
    suspicious = jnp.where((n_top2 != float(k)) | (n_pairs != float(k)), 1.0, 0.0)
    cat = lambda xs: jnp.concatenate(xs, axis=0)
    return cat(rank2), cat(count1), cat(e1), cat(e2), suspicious


def _select_exact(s1, s2, pos, neg):
    k = PEER_TOPK
    rank1, a = _top_ranks(s1)
    rank2, b = _top_ranks(s2)
    pieces = [a[0:1, :] + b]
    for i in range(1, 8):
        pieces.append(a[i:i + 1, :] + b[0:8, :])
    pieces.append(a[8:k, :] + b[0:1, :])
    cand = jnp.concatenate(pieces, axis=0) + neg
    work = cand
    chosen = jnp.zeros(cand.shape, F32)
    for r in range(k):
        m = jnp.max(work, axis=0, keepdims=True)
        first = jnp.min(jnp.where(work == m, pos, float(k * k)), axis=0, keepdims=True)
        sel = pos == first
        chosen = jnp.where(sel, 1.0, chosen)
        work = jnp.where(sel, -jnp.inf, work)
    top = a[0:1, :] + b[0:1, :]
    z = jnp.sum(jnp.where(chosen > 0.0, jnp.exp(cand - top), 0.0), axis=0, keepdims=True)
    row8 = lax.broadcasted_iota(jnp.int32, (8, LANES), 0)
    cnt_lo = jnp.where(row8 == 0, jnp.sum(chosen[0:k, :], axis=0, keepdims=True), 0.0)
    for i in range(1, 8):
        lo = k + 8 * (i - 1)
        cnt_lo = jnp.where(row8 == i, jnp.sum(chosen[lo:lo + 8, :], axis=0, keepdims=True), cnt_lo)
    cnt = jnp.concatenate([cnt_lo, chosen[k + 56:k + 64, :]], axis=0)
    count1 = jnp.zeros(s1.shape, F32)
    for r in range(k):
        count1 = jnp.where(rank1 == float(r), cnt[r:r + 1, :], count1)
    return rank2, count1, jnp.exp(s1 - a[0:1, :]) / z, jnp.exp(s2 - b[0:1, :])


def _select_kernel(qt_ref, k1_ref, k2_ref, pos_ref, neg_ref, r2_ref, e2_ref, c1_ref, e1_ref, s1_ref, s2_ref):
    half = PEER_NKEYS
    s1_ref[...] = _dot(k1_ref[0], qt_ref[0:half, :])
    s2_ref[...] = _dot(k2_ref[0], qt_ref[half:2 * half, :])

    groups_per_iter = 2

    def body(it, carry):
        groups = [it * groups_per_iter + i for i in range(groups_per_iter)]
        lanes = [pl.ds(pl.multiple_of(g * LANES, LANES), LANES) for g in groups]

        def emit(g, ln, rank2, count1, e1, e2):
            r2_ref[0, :, ln] = rank2.astype(r2_ref.dtype)
            c1_ref[0, g] = count1
            e1_ref[0, g] = e1
            e2_ref[0, :, ln] = e2.astype(e2_ref.dtype)

        fast = [_select_fast(s1_ref[:, ln], s2_ref[:, ln]) for ln in lanes]
        for g, ln, (*tables, _) in zip(groups, lanes, fast):
            emit(g, ln, *tables)
        for g, ln, (*_, suspicious) in zip(groups, lanes, fast):
            @pl.when(jnp.max(suspicious) > 0.0)
            def _():
                emit(g, ln, *_select_exact(s1_ref[:, ln], s2_ref[:, ln], pos_ref[...], neg_ref[...]))

        return carry

    lax.fori_loop(0, s1_ref.shape[1] // (LANES * groups_per_iter), body, 0)


def _select(qt, k1, k2, *, ts=1024):
    nq, t = qt.shape
    h = PEER_HEADS
    nk = PEER_NKEYS
    pos, neg = _candidate_tables()
    out_spec = pl.BlockSpec((1, nk, ts), lambda i, j: (j, 0, i))
    row_spec = pl.BlockSpec((1, ts // LANES, nk, LANES), lambda i, j: (j, i, 0, 0))
    return pl.pallas_call(
        _select_kernel,
        grid=(t // ts, h),
        in_specs=[
            pl.BlockSpec((nq // h, ts), lambda i, j: (j, i)),
            pl.BlockSpec((1, nk, nk), lambda i, j: (j, 0, 0)),
            pl.BlockSpec((1, nk, nk), lambda i, j: (j, 0, 0)),
            pl.BlockSpec(pos.shape, lambda i, j: (0, 0)),
            pl.BlockSpec(neg.shape, lambda i, j: (0, 0)),
        ],
        out_specs=[out_spec, out_spec, row_spec, row_spec],
        out_shape=[
            jax.ShapeDtypeStruct((h, nk, t), BF16),
            jax.ShapeDtypeStruct((h, nk, t), BF16),
            jax.ShapeDtypeStruct((h, t // LANES, nk, LANES), F32),
            jax.ShapeDtypeStruct((h, t // LANES, nk, LANES), F32),
        ],
        scratch_shapes=[pltpu.VMEM((nk, ts), F32), pltpu.VMEM((nk, ts), F32)],
        compiler_params=_params("parallel", "parallel"),
        name="peer_select",
    )(qt, k1, k2, jnp.asarray(pos), jnp.asarray(neg))


def _dense_kernel(a2t_ref, u_ref, unext_ref, vt_ref, r2_ref, e2_ref, c1_ref, e1_ref, h1_ref,
                  p_ref, gple_ref, wproj_ref, wgate_ref, gfin_ref, o_ref, acc_ref,
                  act_even_ref, act_odd_ref, *, n_pieces, last):
    j = pl.program_id(1)
    nk = PEER_NKEYS
    te = u_ref.shape[0]
    tt = a2t_ref.shape[1]
    rows = te // n_pieces
    pack = 16

    def activations(lhs):
        return _dot(lhs.astype(BF16), a2t_ref[...]).astype(BF16)

    @pl.when(j == 0)
    def _():
        acc_ref[...] = jnp.zeros_like(acc_ref)
        act_even_ref[...] = activations(u_ref[0:rows, :])

    def gated_piece(p, act):
        pieces = []
        for kk in range(rows // nk):
            i1 = (j * n_pieces + p) * (rows // nk) + kk

            def key_row(ref, h):
                parts = [jnp.broadcast_to(ref[h, g, pl.ds(i1, 1), :], (pack, LANES)) for g in range(tt // LANES)]
                row = jnp.concatenate(parts, axis=1).astype(BF16)
                return jnp.concatenate([row] * (nk // pack), axis=0)

            w = None
            for h in range(PEER_HEADS):
                term = jnp.where(r2_ref[h] < key_row(c1_ref, h), e2_ref[h], jnp.zeros((), BF16)) * key_row(e1_ref, h)
                w = term if w is None else w + term
            a = act[kk * nk:(kk + 1) * nk, :]
            half = jnp.asarray(0.5, a.dtype)
            gelu = (half * a) * (jnp.asarray(1.0, a.dtype) + lax.erf(a * jnp.asarray(2.0 ** -0.5, a.dtype)))
            pieces.append(w * gelu)
        return jnp.concatenate(pieces, axis=0)

    def step(act_ref, next_act_ref):
        act = act_ref[...]
        total = None
        for p in range(n_pieces):
            if p + 1 < n_pieces:
                next_act = activations(u_ref[(p + 1) * rows:(p + 2) * rows, :])
            part = _dot(vt_ref[:, p * rows:(p + 1) * rows], gated_piece(p, act))
            total = part if total is None else total + part
            act = next_act
        acc_ref[...] += total
        next_act_ref[...] = activations(unext_ref[...])

    step(act_even_ref, act_even_ref)

    @pl.when(j == pl.num_programs(1) - 1)
    def _():
        h2 = h1_ref[...] + acc_ref[...].T
        ple = _dot(p_ref[...].astype(BF16), wproj_ref[...])
        gate = jax.nn.sigmoid(_dot(_rms(h2, gple_ref[...]).astype(BF16), wgate_ref[...]))
        h3 = h2 + gate * ple
        o_ref[...] = _rms(h3, gfin_ref[...]) if last else h3


def _dense(a2t, u, vt, r2, e2, c1, e1, h1, p, g_ple, w_proj, w_gate, g_final, *, last, tt=512, te=1024, n_pieces=4):
    d, t = a2t.shape
    ne = u.shape[0]
    pd = p.shape[1]
    h, nk = PEER_HEADS, PEER_NKEYS
    nb = ne // te
    tab = pl.BlockSpec((h, nk, tt), lambda i, j: (0, 0, i))
    rowtab = pl.BlockSpec((h, tt // LANES, nk, LANES), lambda i, j: (0, i, 0, 0))

    def const(shape):
        return pl.BlockSpec(shape, lambda i, j: (0,) * len(shape), pipeline_mode=pl.Buffered(1))

    return pl.pallas_call(
        functools.partial(_dense_kernel, n_pieces=n_pieces, last=last),
        grid=(t // tt, nb),
        in_specs=[
            pl.BlockSpec((d, tt), lambda i, j: (0, i)),
            pl.BlockSpec((te, d), lambda i, j: (j, 0)),
            pl.BlockSpec((te // n_pieces, d), lambda i, j: (jnp.minimum(j + 1, nb - 1) * n_pieces, 0)),
            pl.BlockSpec((d, te), lambda i, j: (0, j)),
            tab, tab, rowtab, rowtab,
            pl.BlockSpec((tt, d), lambda i, j: (i, 0)),
            pl.BlockSpec((tt, pd), lambda i, j: (i, 0)),
            const((1, d)), const((pd, d)), const((d, d)), const((1, d)),
        ],
        out_specs=pl.BlockSpec((tt, d), lambda i, j: (i, 0)),
        out_shape=jax.ShapeDtypeStruct((t, d), F32),
        scratch_shapes=[pltpu.VMEM((d, tt), F32), pltpu.VMEM((te // n_pieces, tt), BF16),
                        pltpu.VMEM((te // n_pieces, tt), BF16)],
        compiler_params=_params("parallel", "arbitrary"),
        name="peer_dense",
    )(a2t, u, u, vt, r2, e2, c1, e1, h1, p, g_ple, w_proj, w_gate, g_final)


def kernel(x, p, g_mix, w_in, conv_w, conv_b, w_ret_up, w_conv_up, w_out, g_ffn, w_peer_q, peer_k1, peer_k2, peer_u, peer_v, g_ple, w_ple_proj, w_ple_gate, g_final):
    b, s, d = x.shape
    t = b * s
    depth = w_in.shape[0]
    tables = _retention_tables(s)
    h = x.reshape(t, d)
    for i in range(depth):
        proj = _inproj(h, g_mix[i][None], w_in[i].astype(BF16))
        ret = _retention(proj, *tables, batch=b, seq=s)
        h1, a2t, qt = _mix(h, ret, proj, conv_w[i], conv_b[i][None], w_ret_up[i].astype(BF16),
                           w_conv_up[i].astype(BF16), w_out[i].astype(BF16), g_ffn[i][None],
                           w_peer_q[i].T.astype(BF16), seq=s)
        r2, e2, c1, e1 = _select(qt, peer_k1[i].astype(BF16), peer_k2[i].astype(BF16))
        h = _dense(a2t, peer_u[i].astype(BF16), peer_v[i].T.astype(BF16), r2, e2, c1, e1, h1,
                   p[i].reshape(t, -1), g_ple[i][None], w_ple_proj[i].astype(BF16),
                   w_ple_gate[i].astype(BF16), g_final[None], last=i == depth - 1)
    return h.reshape(b, s, d)
```

```python
import functools

import numpy as np
import jax
import jax.numpy as jnp
from jax import lax
from jax.experimental import pallas as pl
from jax.experimental.pallas import tpu as pltpu

F32 = jnp.float32
BF16 = jnp.bfloat16
EPS = 1e-6

RET_HEADS = 4
RET_DK = 128
RET_DV = 256
RET_CHUNK = 128
ROPE_BASE = 10000.0
CONV_K = 3
PEER_HEADS = 8
PEER_NKEYS = 128
PEER_TOPK = 16

VMEM_LIMIT_BYTES = 56 * 1024 * 1024
LANES = 128


def _params(*sem):
    return pltpu.CompilerParams(dimension_semantics=sem, vmem_limit_bytes=VMEM_LIMIT_BYTES)


def _rms(x, g):
    return x * lax.rsqrt(jnp.mean(x * x, axis=-1, keepdims=True) + EPS) * g


def _dot(a, b):
    return jnp.dot(a, b, preferred_element_type=F32)


def _inproj_kernel(x_ref, g_ref, w_ref, o_ref):
    a = _rms(x_ref[...], g_ref[...]).astype(BF16)
    o_ref[...] = _dot(a, w_ref[...]).astype(o_ref.dtype)


def _inproj(x, g, w, *, tt=512):
    t, d = x.shape
    n = w.shape[1]
    return pl.pallas_call(
        _inproj_kernel,
        grid=(t // tt,),
        in_specs=[
            pl.BlockSpec((tt, d), lambda i: (i, 0)),
            pl.BlockSpec((1, d), lambda i: (0, 0)),
            pl.BlockSpec((d, n), lambda i: (0, 0), pipeline_mode=pl.Buffered(1)),
        ],
        out_specs=pl.BlockSpec((tt, n), lambda i: (i, 0)),
        out_shape=jax.ShapeDtypeStruct((t, n), BF16),
        compiler_params=_params("parallel"),
        name="inproj",
    )(x, g, w)


def _ret_kernel(q_ref, k_ref, v_ref, g_ref, cos_ref, sin_ref, dec_ref, xi_ref, zeta_ref, o_ref, st_ref, *, chunk):
    @pl.when(pl.program_id(1) == 0)
    def _():
        st_ref[...] = jnp.zeros_like(st_ref)

    scale = RET_DK ** -0.5
    half = RET_DK // 2

    def body(c, carry):
        r = pl.ds(pl.multiple_of(c * chunk, chunk), chunk)
        cosv = cos_ref[r, :]
        sinv = sin_ref[r, :]
        for bb in range(q_ref.shape[0]):
            for h in range(RET_HEADS):
                qk = slice(h * RET_DK, (h + 1) * RET_DK)
                vv = slice(h * RET_DV, (h + 1) * RET_DV)
                q = q_ref[bb, r, qk].astype(F32)
                k = k_ref[bb, r, qk].astype(F32)
                q = q * cosv + pltpu.roll(q, half, 1) * sinv
                k = (k * cosv + pltpu.roll(k, half, 1) * sinv) * scale
                qb = q.astype(BF16)
                kb = k.astype(BF16)
                v = v_ref[bb, r, vv]
                s = lax.dot_general(qb, kb, (((1,), (1,)), ((), ())), preferred_element_type=F32) * dec_ref[h]
                intra = _dot(s.astype(BF16), v)
                st = st_ref[bb, h]
                cross = _dot(qb, st.astype(BF16)) * xi_ref[h]
                vz = (v.astype(F32) * zeta_ref[h]).astype(BF16)
                kv = lax.dot_general(kb, vz, (((0,), (0,)), ((), ())), preferred_element_type=F32)
                st_ref[bb, h] = st * xi_ref[h, chunk - 1:chunk, :] + kv
                y = intra + cross
                yc = y - jnp.mean(y, axis=-1, keepdims=True)
                yn = yc * lax.rsqrt(jnp.mean(yc * yc, axis=-1, keepdims=True) + EPS)
                g = g_ref[bb, r, vv].astype(F32)
                o_ref[bb, r, vv] = (g * jax.nn.sigmoid(g) * yn).astype(o_ref.dtype)
        return carry

    lax.fori_loop(0, q_ref.shape[1] // chunk, body, 0)


def _retention(proj, cosf, sinf, dec, xi, zeta, *, batch, seq, sb=1024, seqs_per_step=2):
    t, width = proj.shape
    h, c = RET_HEADS, RET_CHUNK
    sb = min(sb, seq)
    bs = seqs_per_step if batch % seqs_per_step == 0 else 1
    qk_w, v_w = h * RET_DK, h * RET_DV
    assert 2 * qk_w == v_w
    proj3 = proj.reshape(batch, seq, width)

    def rows(col):
        return lambda b, s: (b, s, col)

    def const(shape):
        return pl.BlockSpec(shape, lambda b, s: (0,) * len(shape))

    out = pl.pallas_call(
        functools.partial(_ret_kernel, chunk=c),
        grid=(batch // bs, seq // sb),
        in_specs=[
            pl.BlockSpec((bs, sb, qk_w), rows(0)),
            pl.BlockSpec((bs, sb, qk_w), rows(1)),
            pl.BlockSpec((bs, sb, v_w), rows(1)),
            pl.BlockSpec((bs, sb, v_w), rows(2)),
            pl.BlockSpec((sb, RET_DK), lambda b, s: (s, 0)),
            pl.BlockSpec((sb, RET_DK), lambda b, s: (s, 0)),
            const((h, c, c)), const((h, c, RET_DV)), const((h, c, RET_DV)),
        ],
        out_specs=pl.BlockSpec((bs, sb, v_w), lambda b, s: (b, s, 0)),
        out_shape=jax.ShapeDtypeStruct((batch, seq, v_w), BF16),
        scratch_shapes=[pltpu.VMEM((bs, h, RET_DK, RET_DV), F32)],
        compiler_params=_params("parallel", "arbitrary"),
        name="retention",
    )(proj3, proj3, proj3, proj3, cosf, sinf, dec, xi, zeta)
    return out.reshape(t, v_w)


def _retention_tables(seq):
    h, c = RET_HEADS, RET_CHUNK
    half = RET_DK // 2
    inv = ROPE_BASE ** (-jnp.arange(half, dtype=F32) / half)
    ang = jnp.arange(seq, dtype=F32)[:, None] * inv[None, :]
    cos, sin = jnp.cos(ang), jnp.sin(ang)
    cosf = jnp.concatenate([cos, cos], axis=-1)
    sinf = jnp.concatenate([-sin, sin], axis=-1)
    log_gamma = jnp.log1p(-(2.0 ** (-5.0 - jnp.arange(h, dtype=F32))))
    pos = jnp.arange(c, dtype=F32)
    diff = pos[:, None] - pos[None, :]
    causal = diff >= 0
    dec = jnp.where(causal[None], jnp.exp(log_gamma[:, None, None] * jnp.where(causal, diff, 0.0)[None]), 0.0)
    xi = jnp.exp(log_gamma[:, None] * (pos[None, :] + 1.0))
    zeta = jnp.exp(log_gamma[:, None] * (c - 1.0 - pos[None, :]))
    xi = jnp.broadcast_to(xi[:, :, None], (h, c, RET_DV))
    zeta = jnp.broadcast_to(zeta[:, :, None], (h, c, RET_DV))
    return cosf, sinf, dec, xi, zeta


def _mix_kernel(x_ref, ret_ref, cb_ref, cc_ref, ch_ref, gr_ref, gc_ref, cch_ref, chh_ref, cw_ref, cbias_ref,
                wru_ref, wcu_ref, wo_ref, gffn_ref, wqt_ref, h1_ref, a2t_ref, qt_ref, *, tiles_per_seq):
    i = pl.program_id(0)
    u = cc_ref[...].astype(F32) * ch_ref[...].astype(F32)
    halo = cch_ref[...].astype(F32) * chh_ref[...].astype(F32)
    halo = jnp.where(i % tiles_per_seq == 0, 0.0, halo)
    n_halo = halo.shape[0]
    p1 = halo[n_halo - 1:n_halo, :]
    p2 = halo[n_halo - 2:n_halo - 1, :]
    row = lax.broadcasted_iota(jnp.int32, u.shape, 0)
    u1 = jnp.where(row == 0, p1, pltpu.roll(u, 1, 0))
    u2 = jnp.where(row == 0, p2, jnp.where(row == 1, p1, pltpu.roll(u, 2, 0)))
    cw = cw_ref[...]
    z = cw[0:1, :] * u2 + cw[1:2, :] * u1 + cw[2:3, :] * u + cbias_ref[...]
    y_conv = _dot((cb_ref[...].astype(F32) * z).astype(BF16), wcu_ref[...])
    y_ret = _dot(ret_ref[...], wru_ref[...])
    merged = (jax.nn.sigmoid(gr_ref[...].astype(F32)) * y_ret
              + jax.nn.sigmoid(gc_ref[...].astype(F32)) * y_conv)
    h1 = x_ref[...] + _dot(merged.astype(BF16), wo_ref[...])
    h1_ref[...] = h1
    a2t = _rms(h1, gffn_ref[...]).T.astype(BF16)
    a2t_ref[...] = a2t
    qt_ref[...] = _dot(wqt_ref[...], a2t).astype(qt_ref.dtype)


def _mix(x, ret, proj, conv_w, conv_b, w_ret_up, w_conv_up, w_out, g_ffn, wqt, *, seq, tt=512):
    t, d = x.shape
    w = ret.shape[1]
    nq = wqt.shape[0]
    halo = 16
    blk = proj.shape[1] // w
    cb_i, cc_i, ch_i, gr_i, gc_i = blk - 5, blk - 4, blk - 3, blk - 2, blk - 1

    def tile(col):
        return pl.BlockSpec((tt, w), lambda i: (i, col))

    def halo_spec(col):
        return pl.BlockSpec((halo, w), lambda i: (jnp.maximum(i * (tt // halo) - 1, 0), col))

    def const(shape):
        return pl.BlockSpec(shape, lambda i: (0,) * len(shape), pipeline_mode=pl.Buffered(1))

    return pl.pallas_call(
        functools.partial(_mix_kernel, tiles_per_seq=seq // tt),
        grid=(t // tt,),
        in_specs=[
            pl.BlockSpec((tt, d), lambda i: (i, 0)),
            pl.BlockSpec((tt, w), lambda i: (i, 0)),
            tile(cb_i), tile(cc_i), tile(ch_i), tile(gr_i), tile(gc_i),
            halo_spec(cc_i), halo_spec(ch_i),
            const((CONV_K, w)), const((1, w)),
            const((w, d)), const((w, d)), const((d, d)), const((1, d)), const((nq, d)),
        ],
        out_specs=[
            pl.BlockSpec((tt, d), lambda i: (i, 0)),
            pl.BlockSpec((d, tt), lambda i: (0, i)),
            pl.BlockSpec((nq, tt), lambda i: (0, i)),
        ],
        out_shape=[
            jax.ShapeDtypeStruct((t, d), F32),
            jax.ShapeDtypeStruct((d, t), BF16),
            jax.ShapeDtypeStruct((nq, t), BF16),
        ],
        compiler_params=_params("parallel"),
        name="mix",
    )(x, ret, proj, proj, proj, proj, proj, proj, proj, conv_w, conv_b, w_ret_up, w_conv_up, w_out, g_ffn, wqt)


def _candidate_tables():
    k = PEER_TOPK
    pos, valid = [], []
    for j in range(k):
        pos.append(j)
        valid.append(True)
    for i in range(1, 8):
        for j in range(8):
            pos.append(i * k + j)
            valid.append((i + 1) * (j + 1) <= k)
    for i in range(8, k):
        pos.append(i * k)
        valid.append(True)
    pos = np.asarray(pos, np.float32)
    neg = np.where(np.asarray(valid), 0.0, -np.inf).astype(np.float32)
    return (np.ascontiguousarray(np.broadcast_to(pos[:, None], (pos.size, LANES))),
            np.ascontiguousarray(np.broadcast_to(neg[:, None], (neg.size, LANES))))


def _top_ranks(s):
    k = PEER_TOPK
    nkeys = s.shape[0]
    key = lax.broadcasted_iota(jnp.int32, s.shape, 0)
    row = lax.broadcasted_iota(jnp.int32, (k, s.shape[1]), 0)
    rank = jnp.full(s.shape, float(k), F32)
    vals = jnp.zeros((k, s.shape[1]), F32)
    work = s
    for r in range(k):
        m = jnp.max(work, axis=0, keepdims=True)
        first = jnp.min(jnp.where(work == m, key, nkeys), axis=0, keepdims=True)
        sel = key == first
        rank = jnp.where(sel, float(r), rank)
        work = jnp.where(sel, -jnp.inf, work)
        vals = jnp.where(row == r, m, vals)
    return rank, vals


def _sort16_network():
    def merge(lo, hi, r):
        step = r * 2
        if step < hi - lo:
            yield from merge(lo, hi, step)
            yield from merge(lo + r, hi, step)
            yield from [(i, i + r) for i in range(lo + r, hi - r, step)]
        else:
            yield (lo, lo + r)

    def sort(lo, hi):
        if hi - lo >= 1:
            mid = lo + (hi - lo) // 2
            yield from sort(lo, mid)
            yield from sort(mid + 1, hi)
            yield from merge(lo, hi, 1)

    return tuple(sort(0, PEER_TOPK - 1))


def _sublane_allreduce(x, op):
    for shift in (4, 2, 1):
        x = op(x, pltpu.roll(x, shift, 0))
    return x


def _sorted_top_values(s):
    k = PEER_TOPK
    x = [s[8 * i:8 * i + 8, :] for i in range(s.shape[0] // 8)]
    assert len(x) == k
    for i, j in _sort16_network():
        x[i], x[j] = jnp.maximum(x[i], x[j]), jnp.minimum(x[i], x[j])
    for shift in (4, 2, 1):
        p = [pltpu.roll(v, shift, 0) for v in x]
        y = [jnp.maximum(x[r], p[k - 1 - r]) for r in range(k)]
        x = _bitonic_sort(y)
    return x


_SORT10_NETWORK = ((0, 5), (1, 6), (2, 7), (3, 8), (4, 9), (0, 3), (1, 4), (5, 8), (6, 9), (0, 2), (3, 6), (7, 9),
                   (0, 1), (2, 4), (5, 7), (8, 9), (1, 2), (3, 5), (4, 6), (7, 8), (1, 3), (2, 5), (4, 7), (6, 8),
                   (2, 3), (6, 7), (3, 4), (5, 6), (4, 5))


def _bitonic_sort(y):
    d = len(y) // 2
    while d:
        for i in range(len(y)):
            if not i & d:
                y[i], y[i + d] = jnp.maximum(y[i], y[i + d]), jnp.minimum(y[i], y[i + d])
        d //= 2
    return y


def _kth_largest(cand):
    k = PEER_TOPK
    x = list(cand)
    assert len(x) == 10
    for i, j in _SORT10_NETWORK:
        x[i], x[j] = jnp.maximum(x[i], x[j]), jnp.minimum(x[i], x[j])
    x = x + [None] * (k - len(x))
    for shift in (4, 2, 1):
        y = []
        for r in range(k):
            own, other = x[r], x[k - 1 - r]
            other = None if other is None else pltpu.roll(other, shift, 0)
            if own is None or other is None:
                y.append(own if other is None else other)
            else:
                y.append(jnp.maximum(own, other))
        assert all(v is not None for v in y)
        if shift == 1:
            tau = y[0]
            for v in y[1:]:
                tau = jnp.minimum(tau, v)
            return tau
        x = _bitonic_sort(y)


def _count_leading(b, pred):
    m8 = pred(b[7])
    m4 = pred(jnp.where(m8, b[11], b[3]))
    m2 = pred(jnp.where(m8, jnp.where(m4, b[13], b[9]), jnp.where(m4, b[5], b[1])))
    m1 = pred(jnp.where(m8, jnp.where(m4, jnp.where(m2, b[14], b[12]), jnp.where(m2, b[10], b[8])),
                        jnp.where(m4, jnp.where(m2, b[6], b[4]), jnp.where(m2, b[2], b[0]))))
    cnt = (jnp.where(m8, 8.0, 0.0) + jnp.where(m4, 4.0, 0.0)) + (jnp.where(m2, 2.0, 0.0) + jnp.where(m1, 1.0, 0.0))
    return jnp.where(pred(b[15]), 16.0, cnt)


def _select_fast(s1, s2):
    k = PEER_TOPK
    assert k == 16
    n = s1.shape[1]
    a = _sorted_top_values(s1)
    b = _sorted_top_values(s2)
    row = lax.broadcasted_iota(jnp.int32, (8, n), 0)

    def rows_of(vals):
        out = vals[0]
        for r in range(1, 8):
            out = jnp.where(row == r, vals[r], out)
        return out

    b_lo, b_hi, a_hi = rows_of(b[0:8]), rows_of(b[8:16]), rows_of(a[8:16])
    neg_inf = jnp.float32(-jnp.inf)
    cand = [a[0] + b_lo, a[0] + b_hi]
    for i in range(1, 8):
        cand.append(jnp.where(row < k // (i + 1), a[i] + b_lo, neg_inf))
    cand.append(a_hi + b[0])
    tau = _kth_largest(cand)
    top = a[0] + b[0]
    z = None
    for c in cand:
        term = jnp.where(c >= tau, jnp.exp(c - top), 0.0)
        z = term if z is None else z + term
    inv_z = 1.0 / _sublane_allreduce(z, jnp.add)

    rank2, count1, e1, e2 = [], [], [], []
    n_top2 = None
    n_pairs = None
    for i in range(s1.shape[0] // 8):
        x1 = s1[8 * i:8 * i + 8, :]
        x2 = s2[8 * i:8 * i + 8, :]
        r2 = _count_leading(b, lambda t: t > x2)
        c1 = _count_leading(b, lambda t: x1 + t >= tau)
        rank2.append(r2)
        count1.append(c1)
        e1.append(jnp.exp(x1 - a[0]) * inv_z)
        e2.append(jnp.exp(x2 - b[0]))
        in_top = jnp.where(r2 < float(k), 1.0, 0.0)
        n_top2 = in_top if n_top2 is None else n_top2 + in_top
        n_pairs = c1 if n_pairs is None else n_pairs + c1
    n_top2 = _sublane_allreduce(n_top2, jnp.add)
    n_pairs = _sublane_allreduce(n_pairs, jnp.add)
    suspicious = jnp.where((n_top2 != float(k)) | (n_pairs != float(k)), 1.0, 0.0)
    cat = lambda xs: jnp.concatenate(xs, axis=0)
    return cat(rank2), cat(count1), cat(e1), cat(e2), suspicious


def _select_exact(s1, s2, pos, neg):
    k = PEER_TOPK
    rank1, a = _top_ranks(s1)
    rank2, b = _top_ranks(s2)
    pieces = [a[0:1, :] + b]
    for i in range(1, 8):
        pieces.append(a[i:i + 1, :] + b[0:8, :])
    pieces.append(a[8:k, :] + b[0:1, :])
    cand = jnp.concatenate(pieces, axis=0) + neg
    work = cand
    chosen = jnp.zeros(cand.shape, F32)
    for r in range(k):
        m = jnp.max(work, axis=0, keepdims=True)
        first = jnp.min(jnp.where(work == m, pos, float(k * k)), axis=0, keepdims=True)
        sel = pos == first
        chosen = jnp.where(sel, 1.0, chosen)
        work = jnp.where(sel, -jnp.inf, work)
    top = a[0:1, :] + b[0:1, :]
    z = jnp.sum(jnp.where(chosen > 0.0, jnp.exp(cand - top), 0.0), axis=0, keepdims=True)
    row8 = lax.broadcasted_iota(jnp.int32, (8, LANES), 0)
    cnt_lo = jnp.where(row8 == 0, jnp.sum(chosen[0:k, :], axis=0, keepdims=True), 0.0)
    for i in range(1, 8):
        lo = k + 8 * (i - 1)
        cnt_lo = jnp.where(row8 == i, jnp.sum(chosen[lo:lo + 8, :], axis=0, keepdims=True), cnt_lo)
    cnt = jnp.concatenate([cnt_lo, chosen[k + 56:k + 64, :]], axis=0)
    count1 = jnp.zeros(s1.shape, F32)
    for r in range(k):
        count1 = jnp.where(rank1 == float(r), cnt[r:r + 1, :], count1)
    return rank2, count1, jnp.exp(s1 - a[0:1, :]) / z, jnp.exp(s2 - b[0:1, :])


def _select_kernel(qt_ref, k1_ref, k2_ref, pos_ref, neg_ref, r2_ref, e2_ref, c1_ref, e1_ref, s1_ref, s2_ref):
    half = PEER_NKEYS
    s1_ref[...] = _dot(k1_ref[0], qt_ref[0:half, :])
    s2_ref[...] = _dot(k2_ref[0], qt_ref[half:2 * half, :])

    groups_per_iter = 2

    def body(it, carry):
        groups = [it * groups_per_iter + i for i in range(groups_per_iter)]
        lanes = [pl.ds(pl.multiple_of(g * LANES, LANES), LANES) for g in groups]

        def emit(g, ln, rank2, count1, e1, e2):
            r2_ref[0, :, ln] = rank2.astype(r2_ref.dtype)
            c1_ref[0, g] = count1
            e1_ref[0, g] = e1
            e2_ref[0, :, ln] = e2.astype(e2_ref.dtype)

        fast = [_select_fast(s1_ref[:, ln], s2_ref[:, ln]) for ln in lanes]
        for g, ln, (*tables, _) in zip(groups, lanes, fast):
            emit(g, ln, *tables)
        for g, ln, (*_, suspicious) in zip(groups, lanes, fast):
            @pl.when(jnp.max(suspicious) > 0.0)
            def _():
                emit(g, ln, *_select_exact(s1_ref[:, ln], s2_ref[:, ln], pos_ref[...], neg_ref[...]))

        return carry

    lax.fori_loop(0, s1_ref.shape[1] // (LANES * groups_per_iter), body, 0)


def _select(qt, k1, k2, *, ts=2048):
    nq, t = qt.shape
    h = PEER_HEADS
    nk = PEER_NKEYS
    pos, neg = _candidate_tables()
    out_spec = pl.BlockSpec((1, nk, ts), lambda i, j: (j, 0, i))
    row_spec = pl.BlockSpec((1, ts // LANES, nk, LANES), lambda i, j: (j, i, 0, 0))
    return pl.pallas_call(
        _select_kernel,
        grid=(t // ts, h),
        in_specs=[
            pl.BlockSpec((nq // h, ts), lambda i, j: (j, i)),
            pl.BlockSpec((1, nk, nk), lambda i, j: (j, 0, 0)),
            pl.BlockSpec((1, nk, nk), lambda i, j: (j, 0, 0)),
            pl.BlockSpec(pos.shape, lambda i, j: (0, 0)),
            pl.BlockSpec(neg.shape, lambda i, j: (0, 0)),
        ],
        out_specs=[out_spec, out_spec, row_spec, row_spec],
        out_shape=[
            jax.ShapeDtypeStruct((h, nk, t), BF16),
            jax.ShapeDtypeStruct((h, nk, t), BF16),
            jax.ShapeDtypeStruct((h, t // LANES, nk, LANES), F32),
            jax.ShapeDtypeStruct((h, t // LANES, nk, LANES), F32),
        ],
        scratch_shapes=[pltpu.VMEM((nk, ts), F32), pltpu.VMEM((nk, ts), F32)],
        compiler_params=_params("parallel", "parallel"),
        name="peer_select",
    )(qt, k1, k2, jnp.asarray(pos), jnp.asarray(neg))


def _dense_kernel(a2t_ref, u_ref, unext_ref, vt_ref, r2_ref, e2_ref, c1_ref, e1_ref, h1_ref,
                  p_ref, gple_ref, wproj_ref, wgate_ref, gfin_ref, o_ref, acc_ref, act_ref, *, n_pieces, last):
    j = pl.program_id(1)
    nk = PEER_NKEYS
    te = u_ref.shape[0]
    tt = a2t_ref.shape[1]
    rows = te // n_pieces
    pack = 16

    def activations(lhs):
        return _dot(lhs, a2t_ref[...]).astype(BF16)

    @pl.when(j == 0)
    def _():
        acc_ref[...] = jnp.zeros_like(acc_ref)
        act_ref[...] = activations(u_ref[0:rows, :])

    def gated_piece(p, act):
        pieces = []
        for kk in range(rows // nk):
            i1 = (j * n_pieces + p) * (rows // nk) + kk

            def key_row(ref, h):
                parts = [jnp.broadcast_to(ref[h, g, pl.ds(i1, 1), :], (pack, LANES)) for g in range(tt // LANES)]
                row = jnp.concatenate(parts, axis=1).astype(BF16)
                return jnp.concatenate([row] * (nk // pack), axis=0)

            w = None
            for h in range(PEER_HEADS):
                term = jnp.where(r2_ref[h] < key_row(c1_ref, h), e2_ref[h], jnp.zeros((), BF16)) * key_row(e1_ref, h)
                w = term if w is None else w + term
            a = act[kk * nk:(kk + 1) * nk, :]
            half = jnp.asarray(0.5, a.dtype)
            gelu = (half * a) * (jnp.asarray(1.0, a.dtype) + lax.erf(a * jnp.asarray(2.0 ** -0.5, a.dtype)))
            pieces.append(w * gelu)
        return jnp.concatenate(pieces, axis=0)

    act = act_ref[...]
    total = None
    for p in range(n_pieces):
        if p + 1 < n_pieces:
            next_act = activations(u_ref[(p + 1) * rows:(p + 2) * rows, :])
        part = _dot(vt_ref[:, p * rows:(p + 1) * rows], gated_piece(p, act))
        total = part if total is None else total + part
        act = next_act
    acc_ref[...] += total
    act_ref[...] = activations(unext_ref[...])

    @pl.when(j == pl.num_programs(1) - 1)
    def _():
        h2 = h1_ref[...] + acc_ref[...].T
        ple = _dot(p_ref[...].astype(BF16), wproj_ref[...])
        gate = jax.nn.sigmoid(_dot(_rms(h2, gple_ref[...]).astype(BF16), wgate_ref[...]))
        h3 = h2 + gate * ple
        o_ref[...] = _rms(h3, gfin_ref[...]) if last else h3


def _dense(a2t, u, vt, r2, e2, c1, e1, h1, p, g_ple, w_proj, w_gate, g_final, *, last, tt=512, te=1024, n_pieces=4):
    d, t = a2t.shape
    ne = u.shape[0]
    pd = p.shape[1]
    h, nk = PEER_HEADS, PEER_NKEYS
    nb = ne // te
    tab = pl.BlockSpec((h, nk, tt), lambda i, j: (0, 0, i))
    rowtab = pl.BlockSpec((h, tt // LANES, nk, LANES), lambda i, j: (0, i, 0, 0))

    def const(shape):
        return pl.BlockSpec(shape, lambda i, j: (0,) * len(shape), pipeline_mode=pl.Buffered(1))

    return pl.pallas_call(
        functools.partial(_dense_kernel, n_pieces=n_pieces, last=last),
        grid=(t // tt, nb),
        in_specs=[
            pl.BlockSpec((d, tt), lambda i, j: (0, i)),
            pl.BlockSpec((te, d), lambda i, j: (j, 0)),
            pl.BlockSpec((te // n_pieces, d), lambda i, j: (jnp.minimum(j + 1, nb - 1) * n_pieces, 0)),
            pl.BlockSpec((d, te), lambda i, j: (0, j)),
            tab, tab, rowtab, rowtab,
            pl.BlockSpec((tt, d), lambda i, j: (i, 0)),
            pl.BlockSpec((tt, pd), lambda i, j: (i, 0)),
            const((1, d)), const((pd, d)), const((d, d)), const((1, d)),
        ],
        out_specs=pl.BlockSpec((tt, d), lambda i, j: (i, 0)),
        out_shape=jax.ShapeDtypeStruct((t, d), F32),
        scratch_shapes=[pltpu.VMEM((d, tt), F32), pltpu.VMEM((te // n_pieces, tt), BF16)],
        compiler_params=_params("parallel", "arbitrary"),
        name="peer_dense",
    )(a2t, u, u, vt, r2, e2, c1, e1, h1, p, g_ple, w_proj, w_gate, g_final)


def kernel(x, p, g_mix, w_in, conv_w, conv_b, w_ret_up, w_conv_up, w_out, g_ffn, w_peer_q, peer_k1, peer_k2, peer_u, peer_v, g_ple, w_ple_proj, w_ple_gate, g_final):
    b, s, d = x.shape
    t = b * s
    depth = w_in.shape[0]
    tables = _retention_tables(s)
    h = x.reshape(t, d)
    for i in range(depth):
        proj = _inproj(h, g_mix[i][None], w_in[i].astype(BF16))
        ret = _retention(proj, *tables, batch=b, seq=s)
        h1, a2t, qt = _mix(h, ret, proj, conv_w[i], conv_b[i][None], w_ret_up[i].astype(BF16),
                           w_conv_up[i].astype(BF16), w_out[i].astype(BF16), g_ffn[i][None],
                           w_peer_q[i].T.astype(BF16), seq=s)
        r2, e2, c1, e1 = _select(qt, peer_k1[i].astype(BF16), peer_k2[i].astype(BF16))
        h = _dense(a2t, peer_u[i].astype(BF16), peer_v[i].T.astype(BF16), r2, e2, c1, e1, h1,
                   p[i].reshape(t, -1), g_ple[i][None], w_ple_proj[i].astype(BF16),
                   w_ple_gate[i].astype(BF16), g_final[None], last=i == depth - 1)
    return h.reshape(b, s, d)
```

```python
import functools

import numpy as np
import jax
import jax.numpy as jnp
from jax import lax
from jax.experimental import pallas as pl
from jax.experimental.pallas import tpu as pltpu

F32 = jnp.float32
BF16 = jnp.bfloat16
EPS = 1e-6

RET_HEADS = 4
RET_DK = 128
RET_DV = 256
RET_CHUNK = 256
ROPE_BASE = 10000.0
CONV_K = 3
PEER_HEADS = 8
PEER_NKEYS = 128
PEER_TOPK = 16

VMEM_LIMIT_BYTES = 56 * 1024 * 1024
LANES = 128


def _params(*sem):
    return pltpu.CompilerParams(dimension_semantics=sem, vmem_limit_bytes=VMEM_LIMIT_BYTES)


def _rms(x, g):
    return x * lax.rsqrt(jnp.mean(x * x, axis=-1, keepdims=True) + EPS) * g


def _dot(a, b):
    return jnp.dot(a, b, preferred_element_type=F32)


def _inproj_kernel(x_ref, g_ref, w_ref, o_ref):
    a = _rms(x_ref[...], g_ref[...]).astype(BF16)
    o_ref[...] = _dot(a, w_ref[...]).astype(o_ref.dtype)


def _inproj(x, g, w, *, tt=512):
    t, d = x.shape
    n = w.shape[1]
    return pl.pallas_call(
        _inproj_kernel,
        grid=(t // tt,),
        in_specs=[
            pl.BlockSpec((tt, d), lambda i: (i, 0)),
            pl.BlockSpec((1, d), lambda i: (0, 0)),
            pl.BlockSpec((d, n), lambda i: (0, 0), pipeline_mode=pl.Buffered(1)),
        ],
        out_specs=pl.BlockSpec((tt, n), lambda i: (i, 0)),
        out_shape=jax.ShapeDtypeStruct((t, n), BF16),
        compiler_params=_params("parallel"),
        name="inproj",
    )(x, g, w)


def _ret_kernel(q_ref, k_ref, v_ref, g_ref, cos_ref, sin_ref, dec_ref, xi_ref, zeta_ref, o_ref, st_ref, *, chunk):
    @pl.when(pl.program_id(1) == 0)
    def _():
        st_ref[...] = jnp.zeros_like(st_ref)

    scale = RET_DK ** -0.5
    half = RET_DK // 2

    def body(c, carry):
        r = pl.ds(pl.multiple_of(c * chunk, chunk), chunk)
        cosv = cos_ref[r, :]
        sinv = sin_ref[r, :]
        for bb in range(q_ref.shape[0]):
            for h in range(RET_HEADS):
                qk = slice(h * RET_DK, (h + 1) * RET_DK)
                vv = slice(h * RET_DV, (h + 1) * RET_DV)
                q = q_ref[bb, r, qk].astype(F32)
                k = k_ref[bb, r, qk].astype(F32)
                q = q * cosv + pltpu.roll(q, half, 1) * sinv
                k = (k * cosv + pltpu.roll(k, half, 1) * sinv) * scale
                qb = q.astype(BF16)
                kb = k.astype(BF16)
                v = v_ref[bb, r, vv]
                s = lax.dot_general(qb, kb, (((1,), (1,)), ((), ())), preferred_element_type=F32) * dec_ref[h]
                intra = _dot(s.astype(BF16), v)
                st = st_ref[bb, h]
                cross = _dot(qb, st.astype(BF16)) * xi_ref[h]
                vz = (v.astype(F32) * zeta_ref[h]).astype(BF16)
                kv = lax.dot_general(kb, vz, (((0,), (0,)), ((), ())), preferred_element_type=F32)
                st_ref[bb, h] = st * xi_ref[h, chunk - 1:chunk, :] + kv
                y = intra + cross
                yc = y - jnp.mean(y, axis=-1, keepdims=True)
                yn = yc * lax.rsqrt(jnp.mean(yc * yc, axis=-1, keepdims=True) + EPS)
                g = g_ref[bb, r, vv].astype(F32)
                o_ref[bb, r, vv] = (g * jax.nn.sigmoid(g) * yn).astype(o_ref.dtype)
        return carry

    lax.fori_loop(0, q_ref.shape[1] // chunk, body, 0)


def _retention(proj, cosf, sinf, dec, xi, zeta, *, batch, seq, sb=1024, seqs_per_step=2):
    t, width = proj.shape
    h, c = RET_HEADS, RET_CHUNK
    sb = min(sb, seq)
    bs = seqs_per_step if batch % seqs_per_step == 0 else 1
    qk_w, v_w = h * RET_DK, h * RET_DV
    assert 2 * qk_w == v_w
    proj3 = proj.reshape(batch, seq, width)

    def rows(col):
        return lambda b, s: (b, s, col)

    def const(shape):
        return pl.BlockSpec(shape, lambda b, s: (0,) * len(shape))

    out = pl.pallas_call(
        functools.partial(_ret_kernel, chunk=c),
        grid=(batch // bs, seq // sb),
        in_specs=[
            pl.BlockSpec((bs, sb, qk_w), rows(0)),
            pl.BlockSpec((bs, sb, qk_w), rows(1)),
            pl.BlockSpec((bs, sb, v_w), rows(1)),
            pl.BlockSpec((bs, sb, v_w), rows(2)),
            pl.BlockSpec((sb, RET_DK), lambda b, s: (s, 0)),
            pl.BlockSpec((sb, RET_DK), lambda b, s: (s, 0)),
            const((h, c, c)), const((h, c, RET_DV)), const((h, c, RET_DV)),
        ],
        out_specs=pl.BlockSpec((bs, sb, v_w), lambda b, s: (b, s, 0)),
        out_shape=jax.ShapeDtypeStruct((batch, seq, v_w), BF16),
        scratch_shapes=[pltpu.VMEM((bs, h, RET_DK, RET_DV), F32)],
        compiler_params=_params("parallel", "arbitrary"),
        name="retention",
    )(proj3, proj3, proj3, proj3, cosf, sinf, dec, xi, zeta)
    return out.reshape(t, v_w)


def _retention_tables(seq):
    h, c = RET_HEADS, RET_CHUNK
    half = RET_DK // 2
    inv = ROPE_BASE ** (-jnp.arange(half, dtype=F32) / half)
    ang = jnp.arange(seq, dtype=F32)[:, None] * inv[None, :]
    cos, sin = jnp.cos(ang), jnp.sin(ang)
    cosf = jnp.concatenate([cos, cos], axis=-1)
    sinf = jnp.concatenate([-sin, sin], axis=-1)
    log_gamma = jnp.log1p(-(2.0 ** (-5.0 - jnp.arange(h, dtype=F32))))
    pos = jnp.arange(c, dtype=F32)
    diff = pos[:, None] - pos[None, :]
    causal = diff >= 0
    dec = jnp.where(causal[None], jnp.exp(log_gamma[:, None, None] * jnp.where(causal, diff, 0.0)[None]), 0.0)
    xi = jnp.exp(log_gamma[:, None] * (pos[None, :] + 1.0))
    zeta = jnp.exp(log_gamma[:, None] * (c - 1.0 - pos[None, :]))
    xi = jnp.broadcast_to(xi[:, :, None], (h, c, RET_DV))
    zeta = jnp.broadcast_to(zeta[:, :, None], (h, c, RET_DV))
    return cosf, sinf, dec, xi, zeta


def _mix_kernel(x_ref, ret_ref, cb_ref, cc_ref, ch_ref, gr_ref, gc_ref, cch_ref, chh_ref, cw_ref, cbias_ref,
                wru_ref, wcu_ref, wo_ref, gffn_ref, wqt_ref, h1_ref, a2t_ref, qt_ref, *, tiles_per_seq):
    i = pl.program_id(0)
    u = cc_ref[...].astype(F32) * ch_ref[...].astype(F32)
    halo = cch_ref[...].astype(F32) * chh_ref[...].astype(F32)
    halo = jnp.where(i % tiles_per_seq == 0, 0.0, halo)
    n_halo = halo.shape[0]
    p1 = halo[n_halo - 1:n_halo, :]
    p2 = halo[n_halo - 2:n_halo - 1, :]
    row = lax.broadcasted_iota(jnp.int32, u.shape, 0)
    u1 = jnp.where(row == 0, p1, pltpu.roll(u, 1, 0))
    u2 = jnp.where(row == 0, p2, jnp.where(row == 1, p1, pltpu.roll(u, 2, 0)))
    cw = cw_ref[...]
    z = cw[0:1, :] * u2 + cw[1:2, :] * u1 + cw[2:3, :] * u + cbias_ref[...]
    y_conv = _dot((cb_ref[...].astype(F32) * z).astype(BF16), wcu_ref[...])
    y_ret = _dot(ret_ref[...], wru_ref[...])
    merged = (jax.nn.sigmoid(gr_ref[...].astype(F32)) * y_ret
              + jax.nn.sigmoid(gc_ref[...].astype(F32)) * y_conv)
    h1 = x_ref[...] + _dot(merged.astype(BF16), wo_ref[...])
    h1_ref[...] = h1
    a2t = _rms(h1, gffn_ref[...]).T.astype(BF16)
    a2t_ref[...] = a2t
    qt_ref[...] = _dot(wqt_ref[...], a2t).astype(qt_ref.dtype)


def _mix(x, ret, proj, conv_w, conv_b, w_ret_up, w_conv_up, w_out, g_ffn, wqt, *, seq, tt=512):
    t, d = x.shape
    w = ret.shape[1]
    nq = wqt.shape[0]
    halo = 16
    blk = proj.shape[1] // w
    cb_i, cc_i, ch_i, gr_i, gc_i = blk - 5, blk - 4, blk - 3, blk - 2, blk - 1

    def tile(col):
        return pl.BlockSpec((tt, w), lambda i: (i, col))

    def halo_spec(col):
        return pl.BlockSpec((halo, w), lambda i: (jnp.maximum(i * (tt // halo) - 1, 0), col))

    def const(shape):
        return pl.BlockSpec(shape, lambda i: (0,) * len(shape), pipeline_mode=pl.Buffered(1))

    return pl.pallas_call(
        functools.partial(_mix_kernel, tiles_per_seq=seq // tt),
        grid=(t // tt,),
        in_specs=[
            pl.BlockSpec((tt, d), lambda i: (i, 0)),
            pl.BlockSpec((tt, w), lambda i: (i, 0)),
            tile(cb_i), tile(cc_i), tile(ch_i), tile(gr_i), tile(gc_i),
            halo_spec(cc_i), halo_spec(ch_i),
            const((CONV_K, w)), const((1, w)),
            const((w, d)), const((w, d)), const((d, d)), const((1, d)), const((nq, d)),
        ],
        out_specs=[
            pl.BlockSpec((tt, d), lambda i: (i, 0)),
            pl.BlockSpec((d, tt), lambda i: (0, i)),
            pl.BlockSpec((nq, tt), lambda i: (0, i)),
        ],
        out_shape=[
            jax.ShapeDtypeStruct((t, d), F32),
            jax.ShapeDtypeStruct((d, t), BF16),
            jax.ShapeDtypeStruct((nq, t), BF16),
        ],
        compiler_params=_params("parallel"),
        name="mix",
    )(x, ret, proj, proj, proj, proj, proj, proj, proj, conv_w, conv_b, w_ret_up, w_conv_up, w_out, g_ffn, wqt)


def _candidate_tables():
    k = PEER_TOPK
    pos, valid = [], []
    for j in range(k):
        pos.append(j)
        valid.append(True)
    for i in range(1, 8):
        for j in range(8):
            pos.append(i * k + j)
            valid.append((i + 1) * (j + 1) <= k)
    for i in range(8, k):
        pos.append(i * k)
        valid.append(True)
    pos = np.asarray(pos, np.float32)
    neg = np.where(np.asarray(valid), 0.0, -np.inf).astype(np.float32)
    return (np.ascontiguousarray(np.broadcast_to(pos[:, None], (pos.size, LANES))),
            np.ascontiguousarray(np.broadcast_to(neg[:, None], (neg.size, LANES))))


def _top_ranks(s):
    k = PEER_TOPK
    nkeys = s.shape[0]
    key = lax.broadcasted_iota(jnp.int32, s.shape, 0)
    row = lax.broadcasted_iota(jnp.int32, (k, s.shape[1]), 0)
    rank = jnp.full(s.shape, float(k), F32)
    vals = jnp.zeros((k, s.shape[1]), F32)
    work = s
    for r in range(k):
        m = jnp.max(work, axis=0, keepdims=True)
        first = jnp.min(jnp.where(work == m, key, nkeys), axis=0, keepdims=True)
        sel = key == first
        rank = jnp.where(sel, float(r), rank)
        work = jnp.where(sel, -jnp.inf, work)
        vals = jnp.where(row == r, m, vals)
    return rank, vals


def _sort16_network():
    def merge(lo, hi, r):
        step = r * 2
        if step < hi - lo:
            yield from merge(lo, hi, step)
            yield from merge(lo + r, hi, step)
            yield from [(i, i + r) for i in range(lo + r, hi - r, step)]
        else:
            yield (lo, lo + r)

    def sort(lo, hi):
        if hi - lo >= 1:
            mid = lo + (hi - lo) // 2
            yield from sort(lo, mid)
            yield from sort(mid + 1, hi)
            yield from merge(lo, hi, 1)

    return tuple(sort(0, PEER_TOPK - 1))


def _sublane_allreduce(x, op):
    for shift in (4, 2, 1):
        x = op(x, pltpu.roll(x, shift, 0))
    return x


def _sorted_top_values(s):
    k = PEER_TOPK
    x = [s[8 * i:8 * i + 8, :] for i in range(s.shape[0] // 8)]
    assert len(x) == k
    for i, j in _sort16_network():
        x[i], x[j] = jnp.maximum(x[i], x[j]), jnp.minimum(x[i], x[j])
    for shift in (4, 2, 1):
        p = [pltpu.roll(v, shift, 0) for v in x]
        y = [jnp.maximum(x[r], p[k - 1 - r]) for r in range(k)]
        x = _bitonic_sort(y)
    return x


_SORT10_NETWORK = ((0, 5), (1, 6), (2, 7), (3, 8), (4, 9), (0, 3), (1, 4), (5, 8), (6, 9), (0, 2), (3, 6), (7, 9),
                   (0, 1), (2, 4), (5, 7), (8, 9), (1, 2), (3, 5), (4, 6), (7, 8), (1, 3), (2, 5), (4, 7), (6, 8),
                   (2, 3), (6, 7), (3, 4), (5, 6), (4, 5))


def _bitonic_sort(y):
    d = len(y) // 2
    while d:
        for i in range(len(y)):
            if not i & d:
                y[i], y[i + d] = jnp.maximum(y[i], y[i + d]), jnp.minimum(y[i], y[i + d])
        d //= 2
    return y


def _kth_largest(cand):
    k = PEER_TOPK
    x = list(cand)
    assert len(x) == 10
    for i, j in _SORT10_NETWORK:
        x[i], x[j] = jnp.maximum(x[i], x[j]), jnp.minimum(x[i], x[j])
    x = x + [None] * (k - len(x))
    for shift in (4, 2, 1):
        y = []
        for r in range(k):
            own, other = x[r], x[k - 1 - r]
            other = None if other is None else pltpu.roll(other, shift, 0)
            if own is None or other is None:
                y.append(own if other is None else other)
            else:
                y.append(jnp.maximum(own, other))
        assert all(v is not None for v in y)
        if shift == 1:
            tau = y[0]
            for v in y[1:]:
                tau = jnp.minimum(tau, v)
            return tau
        x = _bitonic_sort(y)


def _count_leading(b, pred):
    m8 = pred(b[7])
    m4 = pred(jnp.where(m8, b[11], b[3]))
    m2 = pred(jnp.where(m8, jnp.where(m4, b[13], b[9]), jnp.where(m4, b[5], b[1])))
    m1 = pred(jnp.where(m8, jnp.where(m4, jnp.where(m2, b[14], b[12]), jnp.where(m2, b[10], b[8])),
                        jnp.where(m4, jnp.where(m2, b[6], b[4]), jnp.where(m2, b[2], b[0]))))
    cnt = (jnp.where(m8, 8.0, 0.0) + jnp.where(m4, 4.0, 0.0)) + (jnp.where(m2, 2.0, 0.0) + jnp.where(m1, 1.0, 0.0))
    return jnp.where(pred(b[15]), 16.0, cnt)


def _select_fast(s1, s2):
    k = PEER_TOPK
    assert k == 16
    n = s1.shape[1]
    a = _sorted_top_values(s1)
    b = _sorted_top_values(s2)
    row = lax.broadcasted_iota(jnp.int32, (8, n), 0)

    def rows_of(vals):
        out = vals[0]
        for r in range(1, 8):
            out = jnp.where(row == r, vals[r], out)
        return out

    b_lo, b_hi, a_hi = rows_of(b[0:8]), rows_of(b[8:16]), rows_of(a[8:16])
    neg_inf = jnp.float32(-jnp.inf)
    cand = [a[0] + b_lo, a[0] + b_hi]
    for i in range(1, 8):
        cand.append(jnp.where(row < k // (i + 1), a[i] + b_lo, neg_inf))
    cand.append(a_hi + b[0])
    tau = _kth_largest(cand)
    top = a[0] + b[0]
    z = None
    for c in cand:
        term = jnp.where(c >= tau, jnp.exp(c - top), 0.0)
        z = term if z is None else z + term
    inv_z = 1.0 / _sublane_allreduce(z, jnp.add)

    rank2, count1, e1, e2 = [], [], [], []
    n_top2 = None
    n_pairs = None
    for i in range(s1.shape[0] // 8):
        x1 = s1[8 * i:8 * i + 8, :]
        x2 = s2[8 * i:8 * i + 8, :]
        r2 = _count_leading(b, lambda t: t > x2)
        c1 = _count_leading(b, lambda t: x1 + t >= tau)
        rank2.append(r2)
        count1.append(c1)
        e1.append(jnp.exp(x1 - a[0]) * inv_z)
        e2.append(jnp.exp(x2 - b[0]))
        in_top = jnp.where(r2 < float(k), 1.0, 0.0)
        n_top2 = in_top if n_top2 is None else n_top2 + in_top
        n_pairs = c1 if n_pairs is None else n_pairs + c1
    n_top2 = _sublane_allreduce(n_top2, jnp.add)
    n_pairs = _sublane_allreduce(n_pairs, jnp.add)
    suspicious = jnp.where((n_top2 != float(k)) | (n_pairs != float(k)), 1.0, 0.0)
    cat = lambda xs: jnp.concatenate(xs, axis=0)
    return cat(rank2), cat(count1), cat(e1), cat(e2), suspicious


def _select_exact(s1, s2, pos, neg):
    k = PEER_TOPK
    rank1, a = _top_ranks(s1)
    rank2, b = _top_ranks(s2)
    pieces = [a[0:1, :] + b]
    for i in range(1, 8):
        pieces.append(a[i:i + 1, :] + b[0:8, :])
    pieces.append(a[8:k, :] + b[0:1, :])
    cand = jnp.concatenate(pieces, axis=0) + neg
    work = cand
    chosen = jnp.zeros(cand.shape, F32)
    for r in range(k):
        m = jnp.max(work, axis=0, keepdims=True)
        first = jnp.min(jnp.where(work == m, pos, float(k * k)), axis=0, keepdims=True)
        sel = pos == first
        chosen = jnp.where(sel, 1.0, chosen)
        work = jnp.where(sel, -jnp.inf, work)
    top = a[0:1, :] + b[0:1, :]
    z = jnp.sum(jnp.where(chosen > 0.0, jnp.exp(cand - top), 0.0), axis=0, keepdims=True)
    row8 = lax.broadcasted_iota(jnp.int32, (8, LANES), 0)
    cnt_lo = jnp.where(row8 == 0, jnp.sum(chosen[0:k, :], axis=0, keepdims=True), 0.0)
    for i in range(1, 8):
        lo = k + 8 * (i - 1)
        cnt_lo = jnp.where(row8 == i, jnp.sum(chosen[lo:lo + 8, :], axis=0, keepdims=True), cnt_lo)
    cnt = jnp.concatenate([cnt_lo, chosen[k + 56:k + 64, :]], axis=0)
    count1 = jnp.zeros(s1.shape, F32)
    for r in range(k):
        count1 = jnp.where(rank1 == float(r), cnt[r:r + 1, :], count1)
    return rank2, count1, jnp.exp(s1 - a[0:1, :]) / z, jnp.exp(s2 - b[0:1, :])


def _select_kernel(qt_ref, k1_ref, k2_ref, pos_ref, neg_ref, r2_ref, e2_ref, c1_ref, e1_ref, s1_ref, s2_ref):
    half = PEER_NKEYS
    s1_ref[...] = _dot(k1_ref[0], qt_ref[0:half, :])
    s2_ref[...] = _dot(k2_ref[0], qt_ref[half:2 * half, :])

    groups_per_iter = 2

    def body(it, carry):
        groups = [it * groups_per_iter + i for i in range(groups_per_iter)]
        lanes = [pl.ds(pl.multiple_of(g * LANES, LANES), LANES) for g in groups]

        def emit(g, ln, rank2, count1, e1, e2):
            r2_ref[0, :, ln] = rank2.astype(r2_ref.dtype)
            c1_ref[0, g] = count1
            e1_ref[0, g] = e1
            e2_ref[0, :, ln] = e2.astype(e2_ref.dtype)

        fast = [_select_fast(s1_ref[:, ln], s2_ref[:, ln]) for ln in lanes]
        for g, ln, (*tables, _) in zip(groups, lanes, fast):
            emit(g, ln, *tables)
        for g, ln, (*_, suspicious) in zip(groups, lanes, fast):
            @pl.when(jnp.max(suspicious) > 0.0)
            def _():
                emit(g, ln, *_select_exact(s1_ref[:, ln], s2_ref[:, ln], pos_ref[...], neg_ref[...]))

        return carry

    lax.fori_loop(0, s1_ref.shape[1] // (LANES * groups_per_iter), body, 0)


def _select(qt, k1, k2, *, ts=2048):
    nq, t = qt.shape
    h = PEER_HEADS
    nk = PEER_NKEYS
    pos, neg = _candidate_tables()
    out_spec = pl.BlockSpec((1, nk, ts), lambda i, j: (j, 0, i))
    row_spec = pl.BlockSpec((1, ts // LANES, nk, LANES), lambda i, j: (j, i, 0, 0))
    return pl.pallas_call(
        _select_kernel,
        grid=(t // ts, h),
        in_specs=[
            pl.BlockSpec((nq // h, ts), lambda i, j: (j, i)),
            pl.BlockSpec((1, nk, nk), lambda i, j: (j, 0, 0)),
            pl.BlockSpec((1, nk, nk), lambda i, j: (j, 0, 0)),
            pl.BlockSpec(pos.shape, lambda i, j: (0, 0)),
            pl.BlockSpec(neg.shape, lambda i, j: (0, 0)),
        ],
        out_specs=[out_spec, out_spec, row_spec, row_spec],
        out_shape=[
            jax.ShapeDtypeStruct((h, nk, t), BF16),
            jax.ShapeDtypeStruct((h, nk, t), BF16),
            jax.ShapeDtypeStruct((h, t // LANES, nk, LANES), F32),
            jax.ShapeDtypeStruct((h, t // LANES, nk, LANES), F32),
        ],
        scratch_shapes=[pltpu.VMEM((nk, ts), F32), pltpu.VMEM((nk, ts), F32)],
        compiler_params=_params("parallel", "parallel"),
        name="peer_select",
    )(qt, k1, k2, jnp.asarray(pos), jnp.asarray(neg))


def _dense_kernel(a2t_ref, u_ref, unext_ref, vt_ref, r2_ref, e2_ref, c1_ref, e1_ref, h1_ref,
                  p_ref, gple_ref, wproj_ref, wgate_ref, gfin_ref, o_ref, acc_ref, act_ref, *, n_pieces, last):
    j = pl.program_id(1)
    nk = PEER_NKEYS
    te = u_ref.shape[0]
    tt = a2t_ref.shape[1]
    rows = te // n_pieces
    pack = 16

    def activations(lhs):
        return _dot(lhs, a2t_ref[...]).astype(BF16)

    @pl.when(j == 0)
    def _():
        acc_ref[...] = jnp.zeros_like(acc_ref)
        act_ref[...] = activations(u_ref[0:rows, :])

    def gated_piece(p, act):
        pieces = []
        for kk in range(rows // nk):
            i1 = (j * n_pieces + p) * (rows // nk) + kk

            def key_row(ref, h):
                parts = [jnp.broadcast_to(ref[h, g, pl.ds(i1, 1), :], (pack, LANES)) for g in range(tt // LANES)]
                row = jnp.concatenate(parts, axis=1).astype(BF16)
                return jnp.concatenate([row] * (nk // pack), axis=0)

            w = None
            for h in range(PEER_HEADS):
                term = jnp.where(r2_ref[h] < key_row(c1_ref, h), e2_ref[h], jnp.zeros((), BF16)) * key_row(e1_ref, h)
                w = term if w is None else w + term
            a = act[kk * nk:(kk + 1) * nk, :]
            half = jnp.asarray(0.5, a.dtype)
            gelu = (half * a) * (jnp.asarray(1.0, a.dtype) + lax.erf(a * jnp.asarray(2.0 ** -0.5, a.dtype)))
            pieces.append(w * gelu)
        return jnp.concatenate(pieces, axis=0)

    act = act_ref[...]
    total = None
    for p in range(n_pieces):
        if p + 1 < n_pieces:
            next_act = activations(u_ref[(p + 1) * rows:(p + 2) * rows, :])
        part = _dot(vt_ref[:, p * rows:(p + 1) * rows], gated_piece(p, act))
        total = part if total is None else total + part
        act = next_act
    acc_ref[...] += total
    act_ref[...] = activations(unext_ref[...])

    @pl.when(j == pl.num_programs(1) - 1)
    def _():
        h2 = h1_ref[...] + acc_ref[...].T
        ple = _dot(p_ref[...].astype(BF16), wproj_ref[...])
        gate = jax.nn.sigmoid(_dot(_rms(h2, gple_ref[...]).astype(BF16), wgate_ref[...]))
        h3 = h2 + gate * ple
        o_ref[...] = _rms(h3, gfin_ref[...]) if last else h3


def _dense(a2t, u, vt, r2, e2, c1, e1, h1, p, g_ple, w_proj, w_gate, g_final, *, last, tt=512, te=1024, n_pieces=4):
    d, t = a2t.shape
    ne = u.shape[0]
    pd = p.shape[1]
    h, nk = PEER_HEADS, PEER_NKEYS
    nb = ne // te
    tab = pl.BlockSpec((h, nk, tt), lambda i, j: (0, 0, i))
    rowtab = pl.BlockSpec((h, tt // LANES, nk, LANES), lambda i, j: (0, i, 0, 0))

    def const(shape):
        return pl.BlockSpec(shape, lambda i, j: (0,) * len(shape), pipeline_mode=pl.Buffered(1))

    return pl.pallas_call(
        functools.partial(_dense_kernel, n_pieces=n_pieces, last=last),
        grid=(t // tt, nb),
        in_specs=[
            pl.BlockSpec((d, tt), lambda i, j: (0, i)),
            pl.BlockSpec((te, d), lambda i, j: (j, 0)),
            pl.BlockSpec((te // n_pieces, d), lambda i, j: (jnp.minimum(j + 1, nb - 1) * n_pieces, 0)),
            pl.BlockSpec((d, te), lambda i, j: (0, j)),
            tab, tab, rowtab, rowtab,
            pl.BlockSpec((tt, d), lambda i, j: (i, 0)),
            pl.BlockSpec((tt, pd), lambda i, j: (i, 0)),
            const((1, d)), const((pd, d)), const((d, d)), const((1, d)),
        ],
        out_specs=pl.BlockSpec((tt, d), lambda i, j: (i, 0)),
        out_shape=jax.ShapeDtypeStruct((t, d), F32),
        scratch_shapes=[pltpu.VMEM((d, tt), F32), pltpu.VMEM((te // n_pieces, tt), BF16)],
        compiler_params=_params("parallel", "arbitrary"),
        name="peer_dense",
    )(a2t, u, u, vt, r2, e2, c1, e1, h1, p, g_ple, w_proj, w_gate, g_final)


def kernel(x, p, g_mix, w_in, conv_w, conv_b, w_ret_up, w_conv_up, w_out, g_ffn, w_peer_q, peer_k1, peer_k2, peer_u, peer_v, g_ple, w_ple_proj, w_ple_gate, g_final):
    b, s, d = x.shape
    t = b * s
    depth = w_in.shape[0]
    tables = _retention_tables(s)
    h = x.reshape(t, d)
    for i in range(depth):
        proj = _inproj(h, g_mix[i][None], w_in[i].astype(BF16))
        ret = _retention(proj, *tables, batch=b, seq=s)
        h1, a2t, qt = _mix(h, ret, proj, conv_w[i], conv_b[i][None], w_ret_up[i].astype(BF16),
                           w_conv_up[i].astype(BF16), w_out[i].astype(BF16), g_ffn[i][None],
                           w_peer_q[i].T.astype(BF16), seq=s)
        r2, e2, c1, e1 = _select(qt, peer_k1[i].astype(BF16), peer_k2[i].astype(BF16))
        h = _dense(a2t, peer_u[i].astype(BF16), peer_v[i].T.astype(BF16), r2, e2, c1, e1, h1,
                   p[i].reshape(t, -1), g_ple[i][None], w_ple_proj[i].astype(BF16),
                   w_ple_gate[i].astype(BF16), g_final[None], last=i == depth - 1)
    return h.reshape(b, s, d)
```
